```python
import math
import jax
import jax.numpy as jnp
from jax import lax
import numpy as np

D_MODEL = 1024
BATCH = 16
SEQ = 4096
DEPTH = 4

GRID_W = 64
CTX_LEN = 256
N_MIXERS = 3
HEAD_DIM = 64
WA_HEADS = D_MODEL // HEAD_DIM
WA_KV_HEADS = WA_HEADS // 4
WA_GROUP = WA_HEADS // WA_KV_HEADS
WINDOW = 128
BLOCK = 128
DA_HEADS = D_MODEL // (2 * HEAD_DIM)
D_FF = 128 * ((8 * D_MODEL // 3 + 127) // 128)
CONV_W = 3
ROPE_BASE = 10000.0
EPS = 1e-6
F32 = jnp.float32

kernel_name = "hybrid_interleaved_dit_trunk"


def rmsnorm(x, g):
    xf = x.astype(F32)
    y = xf * lax.rsqrt(jnp.mean(xf * xf, axis=-1, keepdims=True) + EPS)
    return (y * g.astype(F32)).astype(x.dtype)


def modulate(xn, shift, scale):
    return xn * (1 + scale) + shift


def dwconv(x, w):
    ch = x.shape[-1]
    return lax.conv_general_dilated(
        x, w[:, None, :].astype(x.dtype), window_strides=(1,),
        padding=((CONV_W // 2, CONV_W // 2),),
        dimension_numbers=("NWC", "WIO", "NWC"), feature_group_count=ch)


def axial_rope_tables(n_tokens):
    rows = n_tokens // GRID_W
    row = jnp.repeat(jnp.arange(rows, dtype=F32), GRID_W)
    col = jnp.tile(jnp.arange(GRID_W, dtype=F32), rows)
    m = HEAD_DIM // 4
    inv_freq = ROPE_BASE ** (-jnp.arange(m, dtype=F32) / m)
    ang = jnp.stack([row, col], axis=-1)[:, :, None] * inv_freq
    return jnp.cos(ang), jnp.sin(ang)


def apply_rope(x, cos, sin):
    m = HEAD_DIM // 4
    bshape = (cos.shape[0],) + (1,) * (x.ndim - 3) + (2, m)
    cos, sin = cos.reshape(bshape), sin.reshape(bshape)
    xf = x.astype(F32).reshape(x.shape[:-1] + (2, 2, m))
    x1, x2 = xf[..., 0, :], xf[..., 1, :]
    out = jnp.stack([x1 * cos - x2 * sin, x2 * cos + x1 * sin], axis=-2)
    return out.reshape(x.shape).astype(x.dtype)


def short_conv_mixer(x, w_in, w_conv, w_out):
    b_gate, c_gate, h = jnp.split(x @ w_in, 3, axis=-1)
    return (b_gate * dwconv(c_gate * h, w_conv)) @ w_out


def window_gqa_mixer(x, xc, cos, sin, w_qkv, q_g, k_g, sink, w_out, ctx_out):
    bsz, n, _ = x.shape
    n_ctx = xc.shape[1]

    def project(t, rope):
        lead = t.shape[:2]
        q, k, v = jnp.split(t @ w_qkv, [WA_HEADS * HEAD_DIM, (WA_HEADS + WA_KV_HEADS) * HEAD_DIM], axis=-1)
        q = rmsnorm(q.reshape(lead + (WA_KV_HEADS, WA_GROUP, HEAD_DIM)), q_g)
        k = rmsnorm(k.reshape(lead + (WA_KV_HEADS, HEAD_DIM)), k_g)
        v = v.reshape(lead + (WA_KV_HEADS, HEAD_DIM))
        if rope:
            q, k = apply_rope(q, cos, sin), apply_rope(k, cos, sin)
        return q, k, v

    q, k, v = project(x, True)
    qc, kc, vc = project(xc, False)
    scale = HEAD_DIM ** -0.5
    sink_hg = sink.astype(F32).reshape(WA_KV_HEADS, WA_GROUP)[None, :, :, None, None]
    n_blocks = n // BLOCK
    span = BLOCK + 2 * WINDOW
    pad = ((0, 0), (WINDOW, WINDOW), (0, 0), (0, 0))
    k_pad, v_pad = jnp.pad(k, pad), jnp.pad(v, pad)
    q_blocks = jnp.moveaxis(q.reshape(bsz, n_blocks, BLOCK, WA_KV_HEADS, WA_GROUP, HEAD_DIM), 1, 0)

    def attend_block(args):
        q_j, j = args
        start = j * BLOCK
        k_j = lax.dynamic_slice_in_dim(k_pad, start, span, axis=1)
        v_j = lax.dynamic_slice_in_dim(v_pad, start, span, axis=1)
        q_pos = start + jnp.arange(BLOCK)
        k_pos = start - WINDOW + jnp.arange(span)
        mask = (jnp.abs(q_pos[:, None] - k_pos[None, :]) <= WINDOW) & (k_pos >= 0) & (k_pos < n)
        s_lat = jnp.einsum("bqhgd,bkhd->bhgqk", q_j, k_j, preferred_element_type=F32) * scale
        s_lat = jnp.where(mask, s_lat, -jnp.inf)
        s_ctx = jnp.einsum("bqhgd,bkhd->bhgqk", q_j, kc, preferred_element_type=F32) * scale
        s_sink = jnp.broadcast_to(sink_hg, s_ctx.shape[:-1] + (1,))
        p = jax.nn.softmax(jnp.concatenate([s_lat, s_ctx, s_sink], axis=-1), axis=-1).astype(v.dtype)
        return (jnp.einsum("bhgqk,bkhd->bqhgd", p[..., :span], v_j)
                + jnp.einsum("bhgqk,bkhd->bqhgd", p[..., span:span + n_ctx], vc))

    o = lax.map(attend_block, (q_blocks, jnp.arange(n_blocks)))
    y = jnp.moveaxis(o, 0, 1).reshape(bsz, n, WA_HEADS * HEAD_DIM) @ w_out
    if not ctx_out:
        return y, None
    s_c = jnp.einsum("bqhgd,bkhd->bhgqk", qc, kc, preferred_element_type=F32) * scale
    s_sink = jnp.broadcast_to(sink_hg, s_c.shape[:-1] + (1,))
    p_c = jax.nn.softmax(jnp.concatenate([s_c, s_sink], axis=-1), axis=-1)[..., :n_ctx].astype(vc.dtype)
    yc = jnp.einsum("bhgqk,bkhd->bqhgd", p_c, vc).reshape(bsz, n_ctx, WA_HEADS * HEAD_DIM) @ w_out
    return y, yc


def diff_attn_mixer(x, xc, cos, sin, w_qkv, q_g, k_g, lq1, lk1, lq2, lk2, sub_g, w_out, layer_idx, ctx_out):
    bsz, n, _ = x.shape
    n_ctx = xc.shape[1]
    lam_init = 0.8 - 0.6 * math.exp(-0.3 * layer_idx)
    lam = (jnp.exp(jnp.sum(lq1.astype(F32) * lk1.astype(F32)))
           - jnp.exp(jnp.sum(lq2.astype(F32) * lk2.astype(F32))) + lam_init)
    scale = HEAD_DIM ** -0.5

    def project(t, rope):
        lead = t.shape[:2]
        q, k, v = jnp.split(t @ w_qkv, 3, axis=-1)
        q = rmsnorm(q.reshape(lead + (DA_HEADS, 2, HEAD_DIM)), q_g)
        k = rmsnorm(k.reshape(lead + (DA_HEADS, 2, HEAD_DIM)), k_g)
        v = v.reshape(lead + (DA_HEADS, 2 * HEAD_DIM))
        if rope:
            q, k = apply_rope(q, cos, sin), apply_rope(k, cos, sin)
        return q, k, v

    def attend(q_j, keys, vals):
        s = jnp.einsum("bqhcd,bkhcd->bhcqk", q_j, keys, preferred_element_type=F32) * scale
        p = jax.nn.softmax(s, axis=-1)
        a = (p[:, :, 0] - lam * p[:, :, 1]).astype(vals.dtype)
        o = jnp.einsum("bhqk,bkhe->bqhe", a, vals)
        return rmsnorm(o, sub_g) * (1.0 - lam_init)

    q, k, v = project(x, True)
    qc, kc, vc = project(xc, False)
    k_all = jnp.concatenate([k, kc], axis=1)
    v_all = jnp.concatenate([v, vc], axis=1)
    n_blocks = n // BLOCK
    q_blocks = jnp.moveaxis(q.reshape(bsz, n_blocks, BLOCK, DA_HEADS, 2, HEAD_DIM), 1, 0)
    o = lax.map(lambda q_j: attend(q_j, k_all, v_all), q_blocks)
    y = jnp.moveaxis(o, 0, 1).reshape(bsz, n, D_MODEL) @ w_out
    if not ctx_out:
        return y, None
    yc = attend(qc, kc, vc).reshape(bsz, n_ctx, D_MODEL) @ w_out
    return y, yc


def conv_ffn(x, w_up, conv_w, conv_b, w_down):
    a, g = jnp.split(x @ w_up, 2, axis=-1)
    g = dwconv(g, conv_w) + conv_b
    return (jax.nn.silu(g) * a) @ w_down


def setup_inputs(seed: int = 0) -> dict:
    key = jax.random.key(seed)
    ks = iter(jax.random.split(key, 256))

    def nrm(shape, scale):
        return jax.random.normal(next(ks), shape, F32) * scale

    def gain(n):
        return 1.0 + nrm((n,), 0.02)

    D = D_MODEL
    p = {}
    p["x"] = nrm((BATCH, SEQ, D), 1.0)
    p["c"] = nrm((BATCH, D), 1.0)
    p["ctx"] = nrm((BATCH, CTX_LEN, D), 1.0)
    p["c_ctx"] = nrm((D,), 1.0)
    for i in range(DEPTH):
        kind = i % N_MIXERS
        pre = "l%d_" % i
        p[pre + "ada_w"] = nrm((D, 6 * D), 0.5 * D ** -0.5)
        p[pre + "ada_b"] = nrm((6 * D,), 0.02)
        p[pre + "norm1"] = gain(D)
        p[pre + "norm2"] = gain(D)
        if kind == 0:
            p[pre + "sc_in"] = nrm((D, 3 * D), D ** -0.5)
            p[pre + "sc_conv"] = nrm((CONV_W, D), CONV_W ** -0.5)
            p[pre + "sc_out"] = nrm((D, D), D ** -0.5)
        elif kind == 1:
            p[pre + "wa_qkv"] = nrm((D, (WA_HEADS + 2 * WA_KV_HEADS) * HEAD_DIM), D ** -0.5)
            p[pre + "wa_qnorm"] = gain(HEAD_DIM)
            p[pre + "wa_knorm"] = gain(HEAD_DIM)
            p[pre + "wa_sink"] = nrm((WA_HEADS,), 0.5)
            p[pre + "wa_out"] = nrm((WA_HEADS * HEAD_DIM, D), (WA_HEADS * HEAD_DIM) ** -0.5)
        else:
            p[pre + "da_qkv"] = nrm((D, 3 * D), D ** -0.5)
            p[pre + "da_qnorm"] = gain(HEAD_DIM)
            p[pre + "da_knorm"] = gain(HEAD_DIM)
            p[pre + "da_lq1"] = nrm((HEAD_DIM,), 0.1)
            p[pre + "da_lk1"] = nrm((HEAD_DIM,), 0.1)
            p[pre + "da_lq2"] = nrm((HEAD_DIM,), 0.1)
            p[pre + "da_lk2"] = nrm((HEAD_DIM,), 0.1)
            p[pre + "da_subln"] = gain(2 * HEAD_DIM)
            p[pre + "da_out"] = nrm((D, D), D ** -0.5)
        p[pre + "ffn_up"] = nrm((D, 2 * D_FF), D ** -0.5)
        p[pre + "ffn_conv_w"] = nrm((CONV_W, D_FF), CONV_W ** -0.5)
        p[pre + "ffn_conv_b"] = nrm((D_FF,), 0.02)
        p[pre + "ffn_down"] = nrm((D_FF, D), D_FF ** -0.5)
    return p


def reference(x, c, ctx, c_ctx,
              l0_ada_w, l0_ada_b, l0_norm1, l0_norm2, l0_sc_in, l0_sc_conv, l0_sc_out,
              l0_ffn_up, l0_ffn_conv_w, l0_ffn_conv_b, l0_ffn_down,
              l1_ada_w, l1_ada_b, l1_norm1, l1_norm2, l1_wa_qkv, l1_wa_qnorm, l1_wa_knorm, l1_wa_sink, l1_wa_out,
              l1_ffn_up, l1_ffn_conv_w, l1_ffn_conv_b, l1_ffn_down,
              l2_ada_w, l2_ada_b, l2_norm1, l2_norm2, l2_da_qkv, l2_da_qnorm, l2_da_knorm,
              l2_da_lq1, l2_da_lk1, l2_da_lq2, l2_da_lk2, l2_da_subln, l2_da_out,
              l2_ffn_up, l2_ffn_conv_w, l2_ffn_conv_b, l2_ffn_down,
              l3_ada_w, l3_ada_b, l3_norm1, l3_norm2, l3_sc_in, l3_sc_conv, l3_sc_out,
              l3_ffn_up, l3_ffn_conv_w, l3_ffn_conv_b, l3_ffn_down):
    cos, sin = axial_rope_tables(x.shape[1])
    commons = [(l0_ada_w, l0_ada_b, l0_norm1, l0_norm2),
               (l1_ada_w, l1_ada_b, l1_norm1, l1_norm2),
               (l2_ada_w, l2_ada_b, l2_norm1, l2_norm2),
               (l3_ada_w, l3_ada_b, l3_norm1, l3_norm2)]
    mixers = [(l0_sc_in, l0_sc_conv, l0_sc_out),
              (l1_wa_qkv, l1_wa_qnorm, l1_wa_knorm, l1_wa_sink, l1_wa_out),
              (l2_da_qkv, l2_da_qnorm, l2_da_knorm, l2_da_lq1, l2_da_lk1, l2_da_lq2, l2_da_lk2,
               l2_da_subln, l2_da_out),
              (l3_sc_in, l3_sc_conv, l3_sc_out)]
    ffns = [(l0_ffn_up, l0_ffn_conv_w, l0_ffn_conv_b, l0_ffn_down),
            (l1_ffn_up, l1_ffn_conv_w, l1_ffn_conv_b, l1_ffn_down),
            (l2_ffn_up, l2_ffn_conv_w, l2_ffn_conv_b, l2_ffn_down),
            (l3_ffn_up, l3_ffn_conv_w, l3_ffn_conv_b, l3_ffn_down)]

    h, hc = x, ctx
    for i in range(DEPTH):
        kind = i % N_MIXERS
        ada_w, ada_b, g1, g2 = commons[i]
        ctx_after = any(j % N_MIXERS != 0 for j in range(i + 1, DEPTH))
        ctx_here = ctx_after or kind != 0

        sh1, sc1, gt1, sh2, sc2, gt2 = [m[:, None, :] for m in jnp.split(jax.nn.silu(c) @ ada_w + ada_b, 6, axis=-1)]
        xn = modulate(rmsnorm(h, g1), sh1, sc1)
        xcn = None
        if ctx_here:
            ch1, cc1, cg1, ch2, cc2, cg2 = jnp.split(jax.nn.silu(c_ctx) @ ada_w + ada_b, 6, axis=-1)
            xcn = modulate(rmsnorm(hc, g1), ch1, cc1)

        if kind == 0:
            y = short_conv_mixer(xn, *mixers[i])
            yc = short_conv_mixer(xcn, *mixers[i]) if ctx_after else None
        elif kind == 1:
            y, yc = window_gqa_mixer(xn, xcn, cos, sin, *mixers[i], ctx_after)
        else:
            y, yc = diff_attn_mixer(xn, xcn, cos, sin, *mixers[i], i, ctx_after)

        h = h + gt1 * y
        h = h + gt2 * conv_ffn(modulate(rmsnorm(h, g2), sh2, sc2), *ffns[i])
        if ctx_after:
            hc = hc + cg1 * yc
            hc = hc + cg2 * conv_ffn(modulate(rmsnorm(hc, g2), ch2, cc2), *ffns[i])
    return h
```

```python
import functools
import math

import jax
import jax.numpy as jnp
from jax import lax
from jax.experimental import pallas as pl
from jax.experimental.pallas import tpu as pltpu

F32 = jnp.float32
BF16 = jnp.bfloat16

LANES = 128
SUBLANES = 8
MXU_DIM = 256
VMEM_LIMIT_BYTES = 56 * 1024 * 1024

GRID_W = 64
HEAD_DIM = 64
WA_GROUP = 4
WINDOW = 128
CONV_W = 3
N_MIXERS = 3
ROPE_BASE = 10000.0
EPS = 1e-6
NEG = -1e30

NT_DIMS = (((1,), (1,)), ((), ()))


def _cparams(sem):
    return pltpu.CompilerParams(dimension_semantics=sem, vmem_limit_bytes=VMEM_LIMIT_BYTES)


def _sigmoid(x):
    return 1.0 / (1.0 + jnp.exp(-x))


def _norm_mod(x, g, shift, scale):
    ms = jnp.mean(x * x, axis=-1, keepdims=True)
    y = x * lax.rsqrt(ms + EPS) * g
    return y * (1.0 + scale) + shift


def _row_tile(s):
    for tm in (512, 256, 128):
        if s % tm == 0:
            return tm
    raise ValueError(f"sequence length {s} must be a multiple of 128")


def _halo_specs(tm, d, s):
    per = tm // SUBLANES
    last = s // SUBLANES - 1
    cur = pl.BlockSpec((1, tm, d), lambda b, t: (b, t, 0))
    prev = pl.BlockSpec((1, SUBLANES, d), lambda b, t: (b, jnp.maximum(t * per - 1, 0), 0))
    nxt = pl.BlockSpec((1, SUBLANES, d), lambda b, t: (b, jnp.minimum((t + 1) * per, last), 0))
    return cur, prev, nxt


def _mod_spec(mods, d):
    if mods.shape[0] == 1:
        return pl.BlockSpec((1, 6, d), lambda b, t: (0, 0, 0))
    return pl.BlockSpec((1, 6, d), lambda b, t: (b, 0, 0))


def _const_spec(shape):
    zeros = (0,) * len(shape)
    return pl.BlockSpec(shape, lambda *_: zeros, pipeline_mode=pl.Buffered(1))


def _ada_kernel(c_ref, w_ref, b_ref, o_ref):
    cv = c_ref[...]
    act = cv * _sigmoid(cv)
    o_ref[...] = jnp.dot(act, w_ref[...], preferred_element_type=F32,
                         precision=lax.Precision.HIGHEST) + b_ref[...]


def _ada(cvec, ada_w, ada_b):
    r, d = cvec.shape
    n = ada_w.shape[1]
    tn = n // 4
    return pl.pallas_call(
        _ada_kernel,
        grid=(n // tn,),
        in_specs=[pl.BlockSpec((r, d), lambda i: (0, 0)),
                  pl.BlockSpec((d, tn), lambda i: (0, i)),
                  pl.BlockSpec((1, tn), lambda i: (0, i))],
        out_specs=pl.BlockSpec((r, tn), lambda i: (0, i)),
        out_shape=jax.ShapeDtypeStruct((r, n), F32),
        compiler_params=_cparams(("arbitrary",)),
        name="ada_mod",
    )(cvec, ada_w, ada_b.reshape(1, n))


def _shift_conv(scr, pieces, tm, w0, w1, w2):
    prev, cur, nxt = pieces
    scr[0:SUBLANES, :] = prev
    scr[SUBLANES:SUBLANES + tm, :] = cur
    scr[SUBLANES + tm:, :] = nxt
    return (w0 * scr[SUBLANES - 1:SUBLANES - 1 + tm, :] + w1 * cur
            + w2 * scr[SUBLANES + 1:SUBLANES + 1 + tm, :])


def _conv_mixer_kernel(h_ref, hp_ref, hn_ref, mod_ref, g_ref, win_ref, wc_ref, wout_ref,
                       o_ref, scr, *, n_tiles):
    t = pl.program_id(1)
    tm, d = h_ref.shape[1], h_ref.shape[2]
    hcur = h_ref[0]
    shift, scale, gate = mod_ref[0, 0:1, :], mod_ref[0, 1:2, :], mod_ref[0, 2:3, :]
    x_ext = jnp.concatenate([hcur, hp_ref[0], hn_ref[0]], axis=0)
    xn = _norm_mod(x_ext, g_ref[...], shift, scale).astype(BF16)
    b_gate = jnp.dot(xn[:tm], win_ref[:, 0:d], preferred_element_type=F32)
    c_gate = jnp.dot(xn, win_ref[:, d:2 * d], preferred_element_type=F32)
    hh = jnp.dot(xn, win_ref[:, 2 * d:3 * d], preferred_element_type=F32)
    ch = c_gate * hh
    prev = jnp.where(t > 0, ch[tm:tm + SUBLANES], 0.0)
    nxt = jnp.where(t < n_tiles - 1, ch[tm + SUBLANES:], 0.0)
    conv = _shift_conv(scr, (prev, ch[:tm], nxt), tm, wc_ref[0:1, :], wc_ref[1:2, :], wc_ref[2:3, :])
    z = (b_gate * conv).astype(BF16)
    y = jnp.dot(z, wout_ref[...], preferred_element_type=F32)
    o_ref[0] = hcur + gate * y


def _conv_mixer(h, mods, g1, w_in, w_conv, w_out):
    b, s, d = h.shape
    tm = _row_tile(s)
    n_tiles = s // tm
    cur, prev, nxt = _halo_specs(tm, d, s)
    return pl.pallas_call(
        functools.partial(_conv_mixer_kernel, n_tiles=n_tiles),
        grid=(b, n_tiles),
        in_specs=[cur, prev, nxt, _mod_spec(mods, d), _const_spec((1, d)),
                  _const_spec((d, 3 * d)), _const_spec((CONV_W, d)), _const_spec((d, d))],
        out_specs=pl.BlockSpec((1, tm, d), lambda bi, t: (bi, t, 0)),
        out_shape=jax.ShapeDtypeStruct((b, s, d), F32),
        scratch_shapes=[pltpu.VMEM((tm + 2 * SUBLANES, d), F32)],
        compiler_params=_cparams(("parallel", "parallel")),
        name="conv_mixer",
    )(h, h, h, mods, g1, w_in, w_conv, w_out)


def _ffn_chunk(dff):
    best = LANES
    for fc in range(LANES, 1408 + 1, LANES):
        if dff % fc == 0:
            best = fc
    return best


def _ffn_kernel(h_ref, hp_ref, hn_ref, mod_ref, g_ref, wup_ref, cw_ref, cb_ref, wdn_ref,
                o_ref, scr, *, n_tiles, fc):
    t = pl.program_id(1)
    tm, d = h_ref.shape[1], h_ref.shape[2]
    dff = wdn_ref.shape[0]
    hcur = h_ref[0]
    shift, scale, gate = mod_ref[0, 3:4, :], mod_ref[0, 4:5, :], mod_ref[0, 5:6, :]
    x_ext = jnp.concatenate([hcur, hp_ref[0], hn_ref[0]], axis=0)
    xn = _norm_mod(x_ext, g_ref[...], shift, scale).astype(BF16)
    acc = jnp.zeros((tm, d), F32)
    for c0 in range(0, dff, fc):
        a = jnp.dot(xn[:tm], wup_ref[:, c0:c0 + fc], preferred_element_type=F32)
        gg = jnp.dot(xn, wup_ref[:, dff + c0:dff + c0 + fc], preferred_element_type=F32)
        prev = jnp.where(t > 0, gg[tm:tm + SUBLANES], 0.0)
        nxt = jnp.where(t < n_tiles - 1, gg[tm + SUBLANES:], 0.0)
        gc = _shift_conv(scr, (prev, gg[:tm], nxt), tm, cw_ref[0:1, c0:c0 + fc],
                         cw_ref[1:2, c0:c0 + fc], cw_ref[2:3, c0:c0 + fc]) + cb_ref[:, c0:c0 + fc]
        u = (gc * _sigmoid(gc) * a).astype(BF16)
        acc = acc + jnp.dot(u, wdn_ref[c0:c0 + fc, :], preferred_element_type=F32)
    o_ref[0] = hcur + gate * acc


def _ffn(h, mods, g2, w_up, conv_w, conv_b, w_down):
    b, s, d = h.shape
    dff = w_down.shape[0]
    tm = _row_tile(s)
    n_tiles = s // tm
    fc = _ffn_chunk(dff)
    cur, prev, nxt = _halo_specs(tm, d, s)
    return pl.pallas_call(
        functools.partial(_ffn_kernel, n_tiles=n_tiles, fc=fc),
        grid=(b, n_tiles),
        in_specs=[cur, prev, nxt, _mod_spec(mods, d), _const_spec((1, d)),
                  _const_spec((d, 2 * dff)), _const_spec((CONV_W, dff)), _const_spec((1, dff)),
                  _const_spec((dff, d))],
        out_specs=pl.BlockSpec((1, tm, d), lambda bi, t: (bi, t, 0)),
        out_shape=jax.ShapeDtypeStruct((b, s, d), F32),
        scratch_shapes=[pltpu.VMEM((tm + 2 * SUBLANES, fc), F32)],
        compiler_params=_cparams(("parallel", "parallel")),
        name="conv_ffn",
    )(h, h, h, mods, g2, w_up, conv_w, conv_b, w_down)


def _head_sum_matrix():
    i = jnp.arange(MXU_DIM) // HEAD_DIM
    return (i[:, None] == i[None, :]).astype(BF16)


def _rope_tables(n_tokens):
    rows = n_tokens // GRID_W
    row = jnp.repeat(jnp.arange(rows, dtype=F32), GRID_W)
    col = jnp.tile(jnp.arange(GRID_W, dtype=F32), rows)
    m = HEAD_DIM // 4
    inv_freq = ROPE_BASE ** (-jnp.arange(m, dtype=F32) / m)
    ang = jnp.stack([row, col], axis=-1)[:, :, None] * inv_freq
    cos, sin = jnp.cos(ang), jnp.sin(ang)
    c64 = jnp.concatenate([cos[:, 0], cos[:, 0], cos[:, 1], cos[:, 1]], axis=-1)
    s64 = jnp.concatenate([-sin[:, 0], sin[:, 0], -sin[:, 1], sin[:, 1]], axis=-1)
    reps = LANES // HEAD_DIM
    return jnp.tile(c64, (1, reps)), jnp.tile(s64, (1, reps))


def _proj_kernel(*refs, n_norm, widths, rope):
    h_ref, mod_ref, g_ref, w_ref, bm_ref, gain_ref = refs[:6]
    pos = 6
    if rope:
        cos_ref, sin_ref = refs[6:8]
        pos = 8
    out_refs = refs[pos:]
    shift, scale = mod_ref[0, 0:1, :], mod_ref[0, 1:2, :]
    xn = _norm_mod(h_ref[0], g_ref[...], shift, scale).astype(BF16)
    u = jnp.dot(xn, w_ref[...], preferred_element_type=F32)

    starts = [sum(widths[:i]) for i in range(len(widths))]

    def store(c0, val):
        for ref, st, wd in zip(out_refs, starts, widths):
            if st <= c0 < st + wd:
                ref[0, :, c0 - st:c0 - st + val.shape[1]] = val.astype(BF16)

    if rope:
        cos_t, sin_t = cos_ref[...], sin_ref[...]
        lane = lax.broadcasted_iota(jnp.int32, (1, LANES), 1)
        upper = (lane & (HEAD_DIM // 4)) != 0
    for c0 in range(0, n_norm, MXU_DIM):
        xb = u[:, c0:c0 + MXU_DIM]
        ss = jnp.dot((xb * xb).astype(BF16), bm_ref[...], preferred_element_type=F32)
        xb = xb * lax.rsqrt(ss * (1.0 / HEAD_DIM) + EPS) * gain_ref[:, c0:c0 + MXU_DIM]
        for c1 in range(0, MXU_DIM, LANES):
            xs = xb[:, c1:c1 + LANES]
            if rope:
                partner = jnp.where(upper, pltpu.roll(xs, HEAD_DIM // 4, 1),
                                    pltpu.roll(xs, LANES - HEAD_DIM // 4, 1))
                xs = xs * cos_t + partner * sin_t
            store(c0 + c1, xs)
    n_total = sum(widths)
    for c0 in range(n_norm, n_total, LANES):
        store(c0, u[:, c0:c0 + LANES])


def _proj(h, mods, g1, w, gains, widths, n_norm, rope_tabs):
    b, s, d = h.shape
    n = w.shape[1]
    tm = _row_tile(s)
    rope = rope_tabs is not None
    in_specs = [pl.BlockSpec((1, tm, d), lambda bi, t: (bi, t, 0)), _mod_spec(mods, d),
                _const_spec((1, d)), _const_spec((d, n)), _const_spec((MXU_DIM, MXU_DIM)),
                _const_spec((1, n_norm))]
    args = [h, mods, g1, w, _head_sum_matrix(), gains]
    if rope:
        in_specs += [pl.BlockSpec((tm, LANES), lambda bi, t: (t, 0))] * 2
        args += list(rope_tabs)
    return pl.pallas_call(
        functools.partial(_proj_kernel, n_norm=n_norm, widths=tuple(widths), rope=rope),
        grid=(b, s // tm),
        in_specs=in_specs,
        out_specs=[pl.BlockSpec((1, tm, wd), lambda bi, t: (bi, t, 0)) for wd in widths],
        out_shape=[jax.ShapeDtypeStruct((b, s, wd), BF16) for wd in widths],
        compiler_params=_cparams(("parallel", "parallel")),
        name="qkv_proj_rope" if rope else "qkv_proj",
    )(*args)


def _outproj_kernel(h_ref, o_ref, mod_ref, w_ref, out_ref):
    gate = mod_ref[0, 2:3, :]
    y = jnp.dot(o_ref[0], w_ref[...], preferred_element_type=F32)
    out_ref[0] = h_ref[0] + gate * y


def _outproj(h, o, mods, w_out):
    b, s, d = h.shape
    k = o.shape[2]
    tm = _row_tile(s)
    return pl.pallas_call(
        _outproj_kernel,
        grid=(b, s // tm),
        in_specs=[pl.BlockSpec((1, tm, d), lambda bi, t: (bi, t, 0)),
                  pl.BlockSpec((1, tm, k), lambda bi, t: (bi, t, 0)),
                  _mod_spec(mods, d), _const_spec((k, d))],
        out_specs=pl.BlockSpec((1, tm, d), lambda bi, t: (bi, t, 0)),
        out_shape=jax.ShapeDtypeStruct((b, s, d), F32),
        compiler_params=_cparams(("parallel", "parallel")),
        name="out_proj_residual",
    )(h, o, mods, w_out)


def _wa_attn_kernel(*refs, n_blocks, band):
    if band:
        sink_ref, q_ref, kp_ref, kc_ref, kn_ref, vp_ref, vc_ref, vn_ref, kx_ref, vx_ref, o_ref = refs
    else:
        sink_ref, q_ref, kx_ref, vx_ref, o_ref = refs
    j = pl.program_id(1)
    tq = q_ref.shape[1]
    n_kv = kx_ref.shape[2] // LANES
    lane = lax.broadcasted_iota(jnp.int32, (1, LANES), 1)
    lo = lane < HEAD_DIM
    lo_bf = jnp.where(lo, 1.0, 0.0).astype(BF16)
    hi_bf = jnp.where(lo, 0.0, 1.0).astype(BF16)
    if band:
        r = lax.broadcasted_iota(jnp.int32, (WA_GROUP * tq, 3 * tq), 0) & (tq - 1)
        c = lax.broadcasted_iota(jnp.int32, (WA_GROUP * tq, 3 * tq), 1)
        in_window = (c - r).astype(jnp.uint32) <= jnp.uint32(2 * WINDOW)
        c_lo = jnp.where(j > 0, 0, tq)
        c_hi = jnp.where(j < n_blocks - 1, 3 * tq, 2 * tq)
        in_seq = (c >= c_lo) & (c < c_hi)
        ok = in_window & in_seq
    q = q_ref[0]
    for kv in range(n_kv):
        cs = slice(kv * LANES, (kv + 1) * LANES)
        qb0 = q[:, (2 * kv) * LANES:(2 * kv + 1) * LANES]
        qb1 = q[:, (2 * kv + 1) * LANES:(2 * kv + 2) * LANES]
        q4 = jnp.concatenate([qb0 * lo_bf, qb0 * hi_bf, qb1 * lo_bf, qb1 * hi_bf], axis=0)
        sk = jnp.concatenate([jnp.full((tq, 1), sink_ref[WA_GROUP * kv + g], F32)
                              for g in range(WA_GROUP)], axis=0)
        s_x = lax.dot_general(q4, kx_ref[0, :, cs], NT_DIMS, preferred_element_type=F32)
        m = jnp.maximum(jnp.max(s_x, axis=-1, keepdims=True), sk)
        if band:
            kb = jnp.concatenate([kp_ref[0, :, cs], kc_ref[0, :, cs], kn_ref[0, :, cs]], axis=0)
            vb = jnp.concatenate([vp_ref[0, :, cs], vc_ref[0, :, cs], vn_ref[0, :, cs]], axis=0)
            s_b = lax.dot_general(q4, kb, NT_DIMS, preferred_element_type=F32)
            s_b = jnp.where(ok, s_b, NEG)
            m = jnp.maximum(m, jnp.max(s_b, axis=-1, keepdims=True))
        p_x = jnp.exp(s_x - m)
        denom = jnp.sum(p_x, axis=-1, keepdims=True) + jnp.exp(sk - m)
        o4 = jnp.dot(p_x.astype(BF16), vx_ref[0, :, cs], preferred_element_type=F32)
        if band:
            p_b = jnp.exp(s_b - m)
            denom = denom + jnp.sum(p_b, axis=-1, keepdims=True)
            o4 = o4 + jnp.dot(p_b.astype(BF16), vb, preferred_element_type=F32)
        o4 = o4 / denom
        o_ref[0, :, (2 * kv) * LANES:(2 * kv + 1) * LANES] = jnp.where(
            lo, o4[0:tq], o4[tq:2 * tq]).astype(BF16)
        o_ref[0, :, (2 * kv + 1) * LANES:(2 * kv + 2) * LANES] = jnp.where(
            lo, o4[2 * tq:3 * tq], o4[3 * tq:4 * tq]).astype(BF16)


def _wa_attn(sink, q, k, v, kx, vx, band):
    b, s, d = q.shape
    nk = kx.shape[2]
    n_ctx = kx.shape[1]
    smem = pl.BlockSpec(memory_space=pltpu.SMEM)
    if band:
        tq = WINDOW
        nb = s // tq
        kv_specs = [pl.BlockSpec((1, tq, nk), lambda bi, j: (bi, jnp.maximum(j - 1, 0), 0)),
                    pl.BlockSpec((1, tq, nk), lambda bi, j: (bi, j, 0)),
                    pl.BlockSpec((1, tq, nk), lambda bi, j: (bi, jnp.minimum(j + 1, nb - 1), 0))]
        in_specs = [smem, pl.BlockSpec((1, tq, d), lambda bi, j: (bi, j, 0))] + kv_specs + kv_specs
        args = [sink, q, k, k, k, v, v, v]
    else:
        tq = _row_tile(s)
        nb = s // tq
        in_specs = [smem, pl.BlockSpec((1, tq, d), lambda bi, j: (bi, j, 0))]
        args = [sink, q]
    in_specs += [pl.BlockSpec((1, n_ctx, nk), lambda bi, j: (bi, 0, 0))] * 2
    args += [kx, vx]
    return pl.pallas_call(
        functools.partial(_wa_attn_kernel, n_blocks=nb, band=band),
        grid=(b, nb),
        in_specs=in_specs,
        out_specs=pl.BlockSpec((1, tq, d), lambda bi, j: (bi, j, 0)),
        out_shape=jax.ShapeDtypeStruct((b, s, d), BF16),
        compiler_params=_cparams(("parallel", "parallel")),
        name="window_gqa" if band else "ctx_gqa",
    )(*args)


DA_TK = 256


def _da_attn_kernel(lam_ref, subg_ref, q_ref, k_ref, v_ref, kx_ref, vx_ref, o_ref, *, lam_init):
    tq = q_ref.shape[1]
    n_lat = k_ref.shape[1] // DA_TK
    lane = lax.broadcasted_iota(jnp.int32, (1, LANES), 1)
    lo = lane < HEAD_DIM
    lo_bf = jnp.where(lo, 1.0, 0.0).astype(BF16)
    hi_bf = jnp.where(lo, 0.0, 1.0).astype(BF16)
    q = q_ref[0]
    qq = jnp.concatenate([q * lo_bf, q * hi_bf], axis=0)

    def step(kblk, vblk, carry):
        m, l, acc = carry
        s = lax.dot_general(qq, kblk, NT_DIMS, preferred_element_type=F32)
        m_new = jnp.maximum(m, jnp.max(s, axis=-1, keepdims=True))
        alpha = jnp.exp(m - m_new)
        p = jnp.exp(s - m_new)
        l = alpha * l + jnp.sum(p, axis=-1, keepdims=True)
        acc = alpha * acc + jnp.dot(p.astype(BF16), vblk, preferred_element_type=F32)
        return m_new, l, acc

    def body(i, carry):
        off = pl.multiple_of(i * DA_TK, DA_TK)
        return step(k_ref[0, pl.ds(off, DA_TK), :], v_ref[0, pl.ds(off, DA_TK), :], carry)

    init = (jnp.full((2 * tq, 1), NEG, F32), jnp.zeros((2 * tq, 1), F32),
            jnp.zeros((2 * tq, LANES), F32))
    carry = lax.fori_loop(0, n_lat, body, init)
    _, l, acc = step(kx_ref[0], vx_ref[0], carry)
    lam = (jnp.exp(jnp.sum(lam_ref[0:1, :] * lam_ref[1:2, :], axis=-1, keepdims=True))
           - jnp.exp(jnp.sum(lam_ref[2:3, :] * lam_ref[3:4, :], axis=-1, keepdims=True)) + lam_init)
    o = acc[:tq] / l[:tq] - lam * (acc[tq:] / l[tq:])
    o = o * lax.rsqrt(jnp.mean(o * o, axis=-1, keepdims=True) + EPS) * subg_ref[...]
    o_ref[0] = (o * (1.0 - lam_init)).astype(BF16)


def _da_attn(lam_rows, sub_g, q, k, v, kx, vx, lam_init):
    b, s, d = q.shape
    n_ctx = kx.shape[1]
    n_heads = d // LANES
    tq = 256
    return pl.pallas_call(
        functools.partial(_da_attn_kernel, lam_init=lam_init),
        grid=(b, n_heads, s // tq),
        in_specs=[pl.BlockSpec((SUBLANES, LANES), lambda bi, hi, qi: (0, 0)),
                  pl.BlockSpec((1, LANES), lambda bi, hi, qi: (0, 0)),
                  pl.BlockSpec((1, tq, LANES), lambda bi, hi, qi: (bi, qi, hi)),
                  pl.BlockSpec((1, s, LANES), lambda bi, hi, qi: (bi, 0, hi)),
                  pl.BlockSpec((1, s, LANES), lambda bi, hi, qi: (bi, 0, hi)),
                  pl.BlockSpec((1, n_ctx, LANES), lambda bi, hi, qi: (bi, 0, hi)),
                  pl.BlockSpec((1, n_ctx, LANES), lambda bi, hi, qi: (bi, 0, hi))],
        out_specs=pl.BlockSpec((1, tq, LANES), lambda bi, hi, qi: (bi, qi, hi)),
        out_shape=jax.ShapeDtypeStruct((b, s, d), BF16),
        compiler_params=_cparams(("parallel", "parallel", "arbitrary")),
        name="diff_attn",
    )(lam_rows, sub_g, q, k, v, kx, vx)


def _dup_heads(w, n_heads):
    d = w.shape[0]
    w = w.reshape(d, n_heads, HEAD_DIM)
    return jnp.concatenate([w, w], axis=-1).reshape(d, n_heads * LANES)


def _wa_layer(h, hc, mods, mods_c, g1, w_qkv, q_g, k_g, sink, w_out, rope_tabs):
    d = h.shape[2]
    n_q = d // HEAD_DIM
    n_kv = n_q // WA_GROUP
    wq = w_qkv[:, :d]
    wk = _dup_heads(w_qkv[:, d:d + n_kv * HEAD_DIM], n_kv)
    wv = _dup_heads(w_qkv[:, d + n_kv * HEAD_DIM:], n_kv)
    w = jnp.concatenate([wq, wk, wv], axis=1).astype(BF16)
    nk = n_kv * LANES
    gains = jnp.concatenate([jnp.tile(q_g, n_q) * (HEAD_DIM ** -0.5), jnp.tile(k_g, 2 * n_kv)])[None, :]
    widths = (d, nk, nk)
    q, k, v = _proj(h, mods, g1, w, gains, widths, d + nk, rope_tabs)
    qc, kc, vc = _proj(hc, mods_c, g1, w, gains, widths, d + nk, None)
    w_out = w_out.astype(BF16)
    o = _wa_attn(sink, q, k, v, kc, vc, band=True)
    oc = _wa_attn(sink, qc, None, None, kc, vc, band=False)
    return _outproj(h, o, mods, w_out), _outproj(hc, oc, mods_c, w_out)


def _da_layer(h, hc, mods, mods_c, g1, w_qkv, q_g, k_g, lq1, lk1, lq2, lk2, sub_g, w_out,
              layer_idx, rope_tabs):
    d = h.shape[2]
    n_qk = 2 * d // HEAD_DIM
    w = w_qkv.astype(BF16)
    gains = jnp.concatenate([jnp.tile(q_g, n_qk // 2) * (HEAD_DIM ** -0.5),
                             jnp.tile(k_g, n_qk // 2)])[None, :]
    widths = (d, d, d)
    q, k, v = _proj(h, mods, g1, w, gains, widths, 2 * d, rope_tabs)
    _, kc, vc = _proj(hc, mods_c, g1, w, gains, widths, 2 * d, None)
    lam_init = 0.8 - 0.6 * math.exp(-0.3 * layer_idx)
    lam_rows = jnp.zeros((SUBLANES, LANES), F32).at[:4, :HEAD_DIM].set(jnp.stack([lq1, lk1, lq2, lk2]))
    o = _da_attn(lam_rows, sub_g[None, :], q, k, v, kc, vc, lam_init)
    return _outproj(h, o, mods, w_out.astype(BF16))


def kernel(x, c, ctx, c_ctx, l0_ada_w, l0_ada_b, l0_norm1, l0_norm2, l0_sc_in, l0_sc_conv, l0_sc_out, l0_ffn_up, l0_ffn_conv_w, l0_ffn_conv_b, l0_ffn_down, l1_ada_w, l1_ada_b, l1_norm1, l1_norm2, l1_wa_qkv, l1_wa_qnorm, l1_wa_knorm, l1_wa_sink, l1_wa_out, l1_ffn_up, l1_ffn_conv_w, l1_ffn_conv_b, l1_ffn_down, l2_ada_w, l2_ada_b, l2_norm1, l2_norm2, l2_da_qkv, l2_da_qnorm, l2_da_knorm, l2_da_lq1, l2_da_lk1, l2_da_lq2, l2_da_lk2, l2_da_subln, l2_da_out, l2_ffn_up, l2_ffn_conv_w, l2_ffn_conv_b, l2_ffn_down, l3_ada_w, l3_ada_b, l3_norm1, l3_norm2, l3_sc_in, l3_sc_conv, l3_sc_out, l3_ffn_up, l3_ffn_conv_w, l3_ffn_conv_b, l3_ffn_down):
    commons = [(l0_ada_w, l0_ada_b, l0_norm1, l0_norm2),
               (l1_ada_w, l1_ada_b, l1_norm1, l1_norm2),
               (l2_ada_w, l2_ada_b, l2_norm1, l2_norm2),
               (l3_ada_w, l3_ada_b, l3_norm1, l3_norm2)]
    mixers = [(l0_sc_in, l0_sc_conv, l0_sc_out),
              (l1_wa_qkv, l1_wa_qnorm, l1_wa_knorm, l1_wa_sink, l1_wa_out),
              (l2_da_qkv, l2_da_qnorm, l2_da_knorm, l2_da_lq1, l2_da_lk1, l2_da_lq2, l2_da_lk2,
               l2_da_subln, l2_da_out),
              (l3_sc_in, l3_sc_conv, l3_sc_out)]
    ffns = [(l0_ffn_up, l0_ffn_conv_w, l0_ffn_conv_b, l0_ffn_down),
            (l1_ffn_up, l1_ffn_conv_w, l1_ffn_conv_b, l1_ffn_down),
            (l2_ffn_up, l2_ffn_conv_w, l2_ffn_conv_b, l2_ffn_down),
            (l3_ffn_up, l3_ffn_conv_w, l3_ffn_conv_b, l3_ffn_down)]
    depth = len(commons)
    bsz, seq, d = x.shape
    rope_tabs = _rope_tables(seq)

    n_rows = -(-(bsz + 1) // SUBLANES) * SUBLANES
    cvec = jnp.zeros((n_rows, d), F32).at[:bsz].set(c).at[bsz].set(c_ctx)

    h, hc = x, ctx
    for i in range(depth):
        kind = i % N_MIXERS
        ada_w, ada_b, g1, g2 = commons[i]
        ctx_after = any(j % N_MIXERS != 0 for j in range(i + 1, depth))
        mods_all = _ada(cvec, ada_w, ada_b).reshape(n_rows, 6, d)
        mods, mods_c = mods_all[:bsz], mods_all[bsz:bsz + 1]
        g1, g2 = g1[None, :], g2[None, :]

        if kind == 0:
            w_in, w_conv, w_out = mixers[i]
            w_in, w_out = w_in.astype(BF16), w_out.astype(BF16)
            h = _conv_mixer(h, mods, g1, w_in, w_conv, w_out)
            if ctx_after:
                hc = _conv_mixer(hc, mods_c, g1, w_in, w_conv, w_out)
        elif kind == 1:
            h, hc_new = _wa_layer(h, hc, mods, mods_c, g1, *mixers[i], rope_tabs)
            if ctx_after:
                hc = hc_new
        else:
            h = _da_layer(h, hc, mods, mods_c, g1, *mixers[i], i, rope_tabs)
            assert not ctx_after

        w_up, conv_w, conv_b, w_down = ffns[i]
        w_up, w_down = w_up.astype(BF16), w_down.astype(BF16)
        conv_b = conv_b[None, :]
        h = _ffn(h, mods, g2, w_up, conv_w, conv_b, w_down)
        if ctx_after:
            hc = _ffn(hc, mods_c, g2, w_up, conv_w, conv_b, w_down)
    return h
```

```python
import functools
import math

import jax
import jax.numpy as jnp
from jax import lax
from jax.experimental import pallas as pl
from jax.experimental.pallas import tpu as pltpu

F32 = jnp.float32
BF16 = jnp.bfloat16

LANES = 128
SUBLANES = 8
MXU_DIM = 256
VMEM_LIMIT_BYTES = 56 * 1024 * 1024

GRID_W = 64
HEAD_DIM = 64
WA_GROUP = 4
WINDOW = 128
CONV_W = 3
N_MIXERS = 3
ROPE_BASE = 10000.0
EPS = 1e-6
NEG = -1e30

NT_DIMS = (((1,), (1,)), ((), ()))


def _cparams(sem):
    return pltpu.CompilerParams(dimension_semantics=sem, vmem_limit_bytes=VMEM_LIMIT_BYTES)


def _sigmoid(x):
    return 1.0 / (1.0 + jnp.exp(-x))


def _norm_mod(x, g, shift, scale):
    ms = jnp.mean(x * x, axis=-1, keepdims=True)
    y = x * lax.rsqrt(ms + EPS) * g
    return y * (1.0 + scale) + shift


def _row_tile(s):
    for tm in (512, 256, 128):
        if s % tm == 0:
            return tm
    raise ValueError(f"sequence length {s} must be a multiple of 128")


def _halo_specs(tm, d, s):
    per = tm // SUBLANES
    last = s // SUBLANES - 1
    cur = pl.BlockSpec((1, tm, d), lambda b, t: (b, t, 0))
    prev = pl.BlockSpec((1, SUBLANES, d), lambda b, t: (b, jnp.maximum(t * per - 1, 0), 0))
    nxt = pl.BlockSpec((1, SUBLANES, d), lambda b, t: (b, jnp.minimum((t + 1) * per, last), 0))
    return cur, prev, nxt


def _mod_spec(mods, d):
    if mods.shape[0] == 1:
        return pl.BlockSpec((1, 6, d), lambda b, t: (0, 0, 0))
    return pl.BlockSpec((1, 6, d), lambda b, t: (b, 0, 0))


def _const_spec(shape):
    zeros = (0,) * len(shape)
    return pl.BlockSpec(shape, lambda *_: zeros, pipeline_mode=pl.Buffered(1))


def _ada_kernel(c_ref, w_ref, b_ref, o_ref):
    cv = c_ref[...]
    act = cv * _sigmoid(cv)
    o_ref[...] = jnp.dot(act, w_ref[...], preferred_element_type=F32,
                         precision=lax.Precision.HIGHEST) + b_ref[...]


def _ada(cvec, ada_w, ada_b):
    r, d = cvec.shape
    n = ada_w.shape[1]
    tn = n // 4
    return pl.pallas_call(
        _ada_kernel,
        grid=(n // tn,),
        in_specs=[pl.BlockSpec((r, d), lambda i: (0, 0)),
                  pl.BlockSpec((d, tn), lambda i: (0, i)),
                  pl.BlockSpec((1, tn), lambda i: (0, i))],
        out_specs=pl.BlockSpec((r, tn), lambda i: (0, i)),
        out_shape=jax.ShapeDtypeStruct((r, n), F32),
        compiler_params=_cparams(("arbitrary",)),
        name="ada_mod",
    )(cvec, ada_w, ada_b.reshape(1, n))


def _shift_conv(scr, pieces, tm, w0, w1, w2):
    prev, cur, nxt = pieces
    scr[0:SUBLANES, :] = prev
    scr[SUBLANES:SUBLANES + tm, :] = cur
    scr[SUBLANES + tm:, :] = nxt
    return (w0 * scr[SUBLANES - 1:SUBLANES - 1 + tm, :] + w1 * cur
            + w2 * scr[SUBLANES + 1:SUBLANES + 1 + tm, :])


def _conv_mixer_kernel(h_ref, hp_ref, hn_ref, mod_ref, g_ref, win_ref, wc_ref, wout_ref,
                       o_ref, scr, *, n_tiles):
    t = pl.program_id(1)
    tm, d = h_ref.shape[1], h_ref.shape[2]
    hcur = h_ref[0]
    shift, scale, gate = mod_ref[0, 0:1, :], mod_ref[0, 1:2, :], mod_ref[0, 2:3, :]
    x_ext = jnp.concatenate([hcur, hp_ref[0], hn_ref[0]], axis=0)
    xn = _norm_mod(x_ext, g_ref[...], shift, scale).astype(BF16)
    b_gate = jnp.dot(xn[:tm], win_ref[:, 0:d], preferred_element_type=F32)
    c_gate = jnp.dot(xn, win_ref[:, d:2 * d], preferred_element_type=F32)
    hh = jnp.dot(xn, win_ref[:, 2 * d:3 * d], preferred_element_type=F32)
    ch = c_gate * hh
    prev = jnp.where(t > 0, ch[tm:tm + SUBLANES], 0.0)
    nxt = jnp.where(t < n_tiles - 1, ch[tm + SUBLANES:], 0.0)
    conv = _shift_conv(scr, (prev, ch[:tm], nxt), tm, wc_ref[0:1, :], wc_ref[1:2, :], wc_ref[2:3, :])
    z = (b_gate * conv).astype(BF16)
    y = jnp.dot(z, wout_ref[...], preferred_element_type=F32)
    o_ref[0] = hcur + gate * y


def _conv_mixer(h, mods, g1, w_in, w_conv, w_out):
    b, s, d = h.shape
    tm = _row_tile(s)
    n_tiles = s // tm
    cur, prev, nxt = _halo_specs(tm, d, s)
    return pl.pallas_call(
        functools.partial(_conv_mixer_kernel, n_tiles=n_tiles),
        grid=(b, n_tiles),
        in_specs=[cur, prev, nxt, _mod_spec(mods, d), _const_spec((1, d)),
                  _const_spec((d, 3 * d)), _const_spec((CONV_W, d)), _const_spec((d, d))],
        out_specs=pl.BlockSpec((1, tm, d), lambda bi, t: (bi, t, 0)),
        out_shape=jax.ShapeDtypeStruct((b, s, d), F32),
        scratch_shapes=[pltpu.VMEM((tm + 2 * SUBLANES, d), F32)],
        compiler_params=_cparams(("parallel", "parallel")),
        name="conv_mixer",
    )(h, h, h, mods, g1, w_in, w_conv, w_out)


def _ffn_chunk(dff):
    best = LANES
    for fc in range(LANES, 1408 + 1, LANES):
        if dff % fc == 0:
            best = fc
    return best


def _ffn_kernel(h_ref, hp_ref, hn_ref, mod_ref, g_ref, wup_ref, cw_ref, cb_ref, wdn_ref,
                o_ref, scr, *, n_tiles, fc):
    t = pl.program_id(1)
    tm, d = h_ref.shape[1], h_ref.shape[2]
    dff = wdn_ref.shape[0]
    hcur = h_ref[0]
    shift, scale, gate = mod_ref[0, 3:4, :], mod_ref[0, 4:5, :], mod_ref[0, 5:6, :]
    x_ext = jnp.concatenate([hcur, hp_ref[0], hn_ref[0]], axis=0)
    xn = _norm_mod(x_ext, g_ref[...], shift, scale).astype(BF16)
    acc = jnp.zeros((tm, d), F32)
    for c0 in range(0, dff, fc):
        a = jnp.dot(xn[:tm], wup_ref[:, c0:c0 + fc], preferred_element_type=F32)
        gg = jnp.dot(xn, wup_ref[:, dff + c0:dff + c0 + fc], preferred_element_type=F32)
        prev = jnp.where(t > 0, gg[tm:tm + SUBLANES], 0.0)
        nxt = jnp.where(t < n_tiles - 1, gg[tm + SUBLANES:], 0.0)
        gc = _shift_conv(scr, (prev, gg[:tm], nxt), tm, cw_ref[0:1, c0:c0 + fc],
                         cw_ref[1:2, c0:c0 + fc], cw_ref[2:3, c0:c0 + fc]) + cb_ref[:, c0:c0 + fc]
        u = (gc * _sigmoid(gc) * a).astype(BF16)
        acc = acc + jnp.dot(u, wdn_ref[c0:c0 + fc, :], preferred_element_type=F32)
    o_ref[0] = hcur + gate * acc


def _ffn(h, mods, g2, w_up, conv_w, conv_b, w_down):
    b, s, d = h.shape
    dff = w_down.shape[0]
    tm = _row_tile(s)
    n_tiles = s // tm
    fc = _ffn_chunk(dff)
    cur, prev, nxt = _halo_specs(tm, d, s)
    return pl.pallas_call(
        functools.partial(_ffn_kernel, n_tiles=n_tiles, fc=fc),
        grid=(b, n_tiles),
        in_specs=[cur, prev, nxt, _mod_spec(mods, d), _const_spec((1, d)),
                  _const_spec((d, 2 * dff)), _const_spec((CONV_W, dff)), _const_spec((1, dff)),
                  _const_spec((dff, d))],
        out_specs=pl.BlockSpec((1, tm, d), lambda bi, t: (bi, t, 0)),
        out_shape=jax.ShapeDtypeStruct((b, s, d), F32),
        scratch_shapes=[pltpu.VMEM((tm + 2 * SUBLANES, fc), F32)],
        compiler_params=_cparams(("parallel", "parallel")),
        name="conv_ffn",
    )(h, h, h, mods, g2, w_up, conv_w, conv_b, w_down)


def _head_sum_matrix():
    i = jnp.arange(MXU_DIM) // HEAD_DIM
    return (i[:, None] == i[None, :]).astype(BF16)


def _rope_tables(n_tokens):
    rows = n_tokens // GRID_W
    row = jnp.repeat(jnp.arange(rows, dtype=F32), GRID_W)
    col = jnp.tile(jnp.arange(GRID_W, dtype=F32), rows)
    m = HEAD_DIM // 4
    inv_freq = ROPE_BASE ** (-jnp.arange(m, dtype=F32) / m)
    ang = jnp.stack([row, col], axis=-1)[:, :, None] * inv_freq
    cos, sin = jnp.cos(ang), jnp.sin(ang)
    c64 = jnp.concatenate([cos[:, 0], cos[:, 0], cos[:, 1], cos[:, 1]], axis=-1)
    s64 = jnp.concatenate([-sin[:, 0], sin[:, 0], -sin[:, 1], sin[:, 1]], axis=-1)
    reps = LANES // HEAD_DIM
    return jnp.tile(c64, (1, reps)), jnp.tile(s64, (1, reps))


def _proj_kernel(*refs, n_norm, widths, vt_width, rope):
    h_ref, mod_ref, g_ref, w_ref, bm_ref, gain_ref = refs[:6]
    pos = 6
    if rope:
        cos_ref, sin_ref = refs[6:8]
        pos = 8
    out_refs = refs[pos:pos + len(widths)]
    vt_ref = refs[pos + len(widths)]
    shift, scale = mod_ref[0, 0:1, :], mod_ref[0, 1:2, :]
    xn = _norm_mod(h_ref[0], g_ref[...], shift, scale).astype(BF16)
    u = jnp.dot(xn, w_ref[...], preferred_element_type=F32)

    starts = [sum(widths[:i]) for i in range(len(widths))]

    def store(c0, val):
        for ref, st, wd in zip(out_refs, starts, widths):
            if st <= c0 < st + wd:
                ref[0, :, c0 - st:c0 - st + val.shape[1]] = val.astype(BF16)

    if rope:
        cos_t, sin_t = cos_ref[...], sin_ref[...]
        lane = lax.broadcasted_iota(jnp.int32, (1, LANES), 1)
        upper = (lane & (HEAD_DIM // 4)) != 0
    for c0 in range(0, n_norm, MXU_DIM):
        xb = u[:, c0:c0 + MXU_DIM]
        ss = jnp.dot((xb * xb).astype(BF16), bm_ref[...], preferred_element_type=F32)
        xb = xb * lax.rsqrt(ss * (1.0 / HEAD_DIM) + EPS) * gain_ref[:, c0:c0 + MXU_DIM]
        for c1 in range(0, MXU_DIM, LANES):
            xs = xb[:, c1:c1 + LANES]
            if rope:
                partner = jnp.where(upper, pltpu.roll(xs, HEAD_DIM // 4, 1),
                                    pltpu.roll(xs, LANES - HEAD_DIM // 4, 1))
                xs = xs * cos_t + partner * sin_t
            store(c0 + c1, xs)
    n_plain = sum(widths)
    for c0 in range(n_norm, n_plain, LANES):
        store(c0, u[:, c0:c0 + LANES])
    for c0 in range(0, vt_width, LANES):
        vt_ref[0, c0:c0 + LANES, :] = u[:, n_plain + c0:n_plain + c0 + LANES].T.astype(BF16)


def _proj(h, mods, g1, w, gains, widths, vt_width, n_norm, rope_tabs):
    b, s, d = h.shape
    n = w.shape[1]
    tm = _row_tile(s)
    rope = rope_tabs is not None
    in_specs = [pl.BlockSpec((1, tm, d), lambda bi, t: (bi, t, 0)), _mod_spec(mods, d),
                _const_spec((1, d)), _const_spec((d, n)), _const_spec((MXU_DIM, MXU_DIM)),
                _const_spec((1, n_norm))]
    args = [h, mods, g1, w, _head_sum_matrix(), gains]
    if rope:
        in_specs += [pl.BlockSpec((tm, LANES), lambda bi, t: (t, 0))] * 2
        args += list(rope_tabs)
    out_specs = [pl.BlockSpec((1, tm, wd), lambda bi, t: (bi, t, 0)) for wd in widths]
    out_specs.append(pl.BlockSpec((1, vt_width, tm), lambda bi, t: (bi, 0, t)))
    out_shape = [jax.ShapeDtypeStruct((b, s, wd), BF16) for wd in widths]
    out_shape.append(jax.ShapeDtypeStruct((b, vt_width, s), BF16))
    return pl.pallas_call(
        functools.partial(_proj_kernel, n_norm=n_norm, widths=tuple(widths), vt_width=vt_width, rope=rope),
        grid=(b, s // tm),
        in_specs=in_specs,
        out_specs=out_specs,
        out_shape=out_shape,
        compiler_params=_cparams(("parallel", "parallel")),
        name="qkv_proj_rope" if rope else "qkv_proj",
    )(*args)


def _outproj_kernel(h_ref, o_ref, mod_ref, w_ref, out_ref):
    gate = mod_ref[0, 2:3, :]
    y = jnp.dot(o_ref[0], w_ref[...], preferred_element_type=F32)
    out_ref[0] = h_ref[0] + gate * y


def _outproj(h, o, mods, w_out):
    b, s, d = h.shape
    k = o.shape[2]
    tm = _row_tile(s)
    return pl.pallas_call(
        _outproj_kernel,
        grid=(b, s // tm),
        in_specs=[pl.BlockSpec((1, tm, d), lambda bi, t: (bi, t, 0)),
                  pl.BlockSpec((1, tm, k), lambda bi, t: (bi, t, 0)),
                  _mod_spec(mods, d), _const_spec((k, d))],
        out_specs=pl.BlockSpec((1, tm, d), lambda bi, t: (bi, t, 0)),
        out_shape=jax.ShapeDtypeStruct((b, s, d), F32),
        compiler_params=_cparams(("parallel", "parallel")),
        name="out_proj_residual",
    )(h, o, mods, w_out)


def _wa_attn_kernel(*refs, n_blocks, band):
    if band:
        sink_ref, q_ref, kp_ref, kc_ref, kn_ref, vp_ref, vc_ref, vn_ref, kx_ref, vxt_ref, o_ref = refs
    else:
        sink_ref, q_ref, kx_ref, vxt_ref, o_ref = refs
    j = pl.program_id(1)
    tq = q_ref.shape[1]
    n_kv = kx_ref.shape[2] // LANES
    cols = WA_GROUP * tq
    lane = lax.broadcasted_iota(jnp.int32, (1, LANES), 1)
    lo = lane < HEAD_DIM
    lo_bf = jnp.where(lo, 1.0, 0.0).astype(BF16)
    hi_bf = jnp.where(lo, 0.0, 1.0).astype(BF16)
    if band:
        r = lax.broadcasted_iota(jnp.int32, (3 * tq, cols), 1) & (tq - 1)
        c = lax.broadcasted_iota(jnp.int32, (3 * tq, cols), 0)
        in_window = (c - r).astype(jnp.uint32) <= jnp.uint32(2 * WINDOW)
        c_lo = jnp.where(j > 0, 0, tq)
        c_hi = jnp.where(j < n_blocks - 1, 3 * tq, 2 * tq)
        ok = in_window & (c >= c_lo) & (c < c_hi)
    q = q_ref[0]
    for kv in range(n_kv):
        cs = slice(kv * LANES, (kv + 1) * LANES)
        rs = slice(kv * HEAD_DIM, (kv + 1) * HEAD_DIM)
        qb0 = q[:, (2 * kv) * LANES:(2 * kv + 1) * LANES]
        qb1 = q[:, (2 * kv + 1) * LANES:(2 * kv + 2) * LANES]
        q4 = jnp.concatenate([qb0 * lo_bf, qb0 * hi_bf, qb1 * lo_bf, qb1 * hi_bf], axis=0)
        sk = jnp.concatenate([jnp.full((1, tq), sink_ref[WA_GROUP * kv + g], F32)
                              for g in range(WA_GROUP)], axis=1)
        s_x = lax.dot_general(kx_ref[0, :, cs], q4, NT_DIMS, preferred_element_type=F32)
        m = jnp.maximum(jnp.max(s_x, axis=0, keepdims=True), sk)
        if band:
            kb = jnp.concatenate([kp_ref[0, :, cs], kc_ref[0, :, cs], kn_ref[0, :, cs]], axis=0)
            s_b = lax.dot_general(kb, q4, NT_DIMS, preferred_element_type=F32)
            s_b = jnp.where(ok, s_b, NEG)
            m = jnp.maximum(m, jnp.max(s_b, axis=0, keepdims=True))
        p_x = jnp.exp(s_x - m)
        denom = jnp.sum(p_x, axis=0, keepdims=True) + jnp.exp(sk - m)
        o_t = jnp.dot(vxt_ref[0, rs, :], p_x.astype(BF16), preferred_element_type=F32)
        if band:
            p_b = jnp.exp(s_b - m)
            denom = denom + jnp.sum(p_b, axis=0, keepdims=True)
            vb_t = jnp.concatenate([vp_ref[0, rs, :], vc_ref[0, rs, :], vn_ref[0, rs, :]], axis=1)
            o_t = o_t + jnp.dot(vb_t, p_b.astype(BF16), preferred_element_type=F32)
        o_t = o_t / denom
        for half in range(2):
            pair = jnp.concatenate([o_t[:, (2 * half) * tq:(2 * half + 1) * tq],
                                    o_t[:, (2 * half + 1) * tq:(2 * half + 2) * tq]], axis=0)
            o_ref[0, :, (2 * kv + half) * LANES:(2 * kv + half + 1) * LANES] = pair.T.astype(BF16)


def _wa_attn(sink, q, k, vt, kx, vxt, band):
    b, s, d = q.shape
    nk = kx.shape[2]
    nv = vxt.shape[1]
    n_ctx = kx.shape[1]
    smem = pl.BlockSpec(memory_space=pltpu.SMEM)
    if band:
        tq = WINDOW
        nb = s // tq
        k_specs = [pl.BlockSpec((1, tq, nk), lambda bi, j: (bi, jnp.maximum(j - 1, 0), 0)),
                   pl.BlockSpec((1, tq, nk), lambda bi, j: (bi, j, 0)),
                   pl.BlockSpec((1, tq, nk), lambda bi, j: (bi, jnp.minimum(j + 1, nb - 1), 0))]
        v_specs = [pl.BlockSpec((1, nv, tq), lambda bi, j: (bi, 0, jnp.maximum(j - 1, 0))),
                   pl.BlockSpec((1, nv, tq), lambda bi, j: (bi, 0, j)),
                   pl.BlockSpec((1, nv, tq), lambda bi, j: (bi, 0, jnp.minimum(j + 1, nb - 1)))]
        in_specs = [smem, pl.BlockSpec((1, tq, d), lambda bi, j: (bi, j, 0))] + k_specs + v_specs
        args = [sink, q, k, k, k, vt, vt, vt]
    else:
        tq = _row_tile(s)
        nb = s // tq
        in_specs = [smem, pl.BlockSpec((1, tq, d), lambda bi, j: (bi, j, 0))]
        args = [sink, q]
    in_specs += [pl.BlockSpec((1, n_ctx, nk), lambda bi, j: (bi, 0, 0)),
                 pl.BlockSpec((1, nv, n_ctx), lambda bi, j: (bi, 0, 0))]
    args += [kx, vxt]
    return pl.pallas_call(
        functools.partial(_wa_attn_kernel, n_blocks=nb, band=band),
        grid=(b, nb),
        in_specs=in_specs,
        out_specs=pl.BlockSpec((1, tq, d), lambda bi, j: (bi, j, 0)),
        out_shape=jax.ShapeDtypeStruct((b, s, d), BF16),
        compiler_params=_cparams(("parallel", "parallel")),
        name="window_gqa" if band else "ctx_gqa",
    )(*args)


DA_TK = 512


def _da_attn_kernel(lam_ref, subg_ref, q_ref, k_ref, vt_ref, kx_ref, vxt_ref, o_ref, s_scr, *, lam_init):
    tq = q_ref.shape[1]
    s_len = k_ref.shape[1]
    n_ctx = kx_ref.shape[1]
    lane = lax.broadcasted_iota(jnp.int32, (1, LANES), 1)
    lo = lane < HEAD_DIM
    lo_bf = jnp.where(lo, 1.0, 0.0).astype(BF16)
    hi_bf = jnp.where(lo, 0.0, 1.0).astype(BF16)
    q = q_ref[0]
    qq = jnp.concatenate([q * lo_bf, q * hi_bf], axis=0)

    chunks = [(k_ref, vt_ref, r0, min(DA_TK, s_len - r0)) for r0 in range(0, s_len, DA_TK)]
    chunks.append((kx_ref, vxt_ref, 0, n_ctx))

    m = jnp.full((1, 2 * tq), NEG, F32)
    base = 0
    for kref, _, r0, n in chunks:
        s = lax.dot_general(kref[0, r0:r0 + n, :], qq, NT_DIMS, preferred_element_type=F32)
        s_scr[base:base + n, :] = s
        m = jnp.maximum(m, jnp.max(s, axis=0, keepdims=True))
        base += n
    l = jnp.zeros((1, 2 * tq), F32)
    acc = jnp.zeros((LANES, 2 * tq), F32)
    base = 0
    for _, vref, r0, n in chunks:
        p = jnp.exp(s_scr[base:base + n, :] - m)
        l = l + jnp.sum(p, axis=0, keepdims=True)
        acc = acc + jnp.dot(vref[0, :, r0:r0 + n], p.astype(BF16), preferred_element_type=F32)
        base += n
    lam = (jnp.exp(jnp.sum(lam_ref[0:1, :] * lam_ref[1:2, :], axis=-1, keepdims=True))
           - jnp.exp(jnp.sum(lam_ref[2:3, :] * lam_ref[3:4, :], axis=-1, keepdims=True)) + lam_init)
    on = acc / l
    o = on[:, :tq] - lam * on[:, tq:]
    o = o * lax.rsqrt(jnp.mean(o * o, axis=0, keepdims=True) + EPS) * subg_ref[...]
    o_ref[0] = (o * (1.0 - lam_init)).T.astype(BF16)


def _da_attn(lam_rows, sub_g, q, k, vt, kx, vxt, lam_init):
    b, s, d = q.shape
    n_ctx = kx.shape[1]
    n_heads = d // LANES
    tq = 256
    return pl.pallas_call(
        functools.partial(_da_attn_kernel, lam_init=lam_init),
        grid=(b, n_heads, s // tq),
        in_specs=[pl.BlockSpec((SUBLANES, LANES), lambda bi, hi, qi: (0, 0)),
                  pl.BlockSpec((LANES, 1), lambda bi, hi, qi: (0, 0)),
                  pl.BlockSpec((1, tq, LANES), lambda bi, hi, qi: (bi, qi, hi)),
                  pl.BlockSpec((1, s, LANES), lambda bi, hi, qi: (bi, 0, hi)),
                  pl.BlockSpec((1, LANES, s), lambda bi, hi, qi: (bi, hi, 0)),
                  pl.BlockSpec((1, n_ctx, LANES), lambda bi, hi, qi: (bi, 0, hi)),
                  pl.BlockSpec((1, LANES, n_ctx), lambda bi, hi, qi: (bi, hi, 0))],
        out_specs=pl.BlockSpec((1, tq, LANES), lambda bi, hi, qi: (bi, qi, hi)),
        out_shape=jax.ShapeDtypeStruct((b, s, d), BF16),
        scratch_shapes=[pltpu.VMEM((s + n_ctx, 2 * tq), F32)],
        compiler_params=_cparams(("parallel", "parallel", "arbitrary")),
        name="diff_attn",
    )(lam_rows, sub_g, q, k, vt, kx, vxt)


def _dup_heads(w, n_heads):
    d = w.shape[0]
    w = w.reshape(d, n_heads, HEAD_DIM)
    return jnp.concatenate([w, w], axis=-1).reshape(d, n_heads * LANES)


def _wa_layer(h, hc, mods, mods_c, g1, w_qkv, q_g, k_g, sink, w_out, rope_tabs):
    d = h.shape[2]
    n_q = d // HEAD_DIM
    n_kv = n_q // WA_GROUP
    wq = w_qkv[:, :d]
    wk = _dup_heads(w_qkv[:, d:d + n_kv * HEAD_DIM], n_kv)
    wv = w_qkv[:, d + n_kv * HEAD_DIM:]
    w = jnp.concatenate([wq, wk, wv], axis=1).astype(BF16)
    nk = n_kv * LANES
    nv = n_kv * HEAD_DIM
    gains = jnp.concatenate([jnp.tile(q_g, n_q) * (HEAD_DIM ** -0.5), jnp.tile(k_g, 2 * n_kv)])[None, :]
    widths = (d, nk)
    q, k, vt = _proj(h, mods, g1, w, gains, widths, nv, d + nk, rope_tabs)
    qc, kc, vc = _proj(hc, mods_c, g1, w, gains, widths, nv, d + nk, None)
    w_out = w_out.astype(BF16)
    o = _wa_attn(sink, q, k, vt, kc, vc, band=True)
    oc = _wa_attn(sink, qc, None, None, kc, vc, band=False)
    return _outproj(h, o, mods, w_out), _outproj(hc, oc, mods_c, w_out)


def _da_layer(h, hc, mods, mods_c, g1, w_qkv, q_g, k_g, lq1, lk1, lq2, lk2, sub_g, w_out,
              layer_idx, rope_tabs):
    d = h.shape[2]
    n_qk = 2 * d // HEAD_DIM
    w = w_qkv.astype(BF16)
    gains = jnp.concatenate([jnp.tile(q_g, n_qk // 2) * (HEAD_DIM ** -0.5),
                             jnp.tile(k_g, n_qk // 2)])[None, :]
    widths = (d, d)
    q, k, vt = _proj(h, mods, g1, w, gains, widths, d, 2 * d, rope_tabs)
    _, kc, vc = _proj(hc, mods_c, g1, w, gains, widths, d, 2 * d, None)
    lam_init = 0.8 - 0.6 * math.exp(-0.3 * layer_idx)
    lam_rows = jnp.zeros((SUBLANES, LANES), F32).at[:4, :HEAD_DIM].set(jnp.stack([lq1, lk1, lq2, lk2]))
    o = _da_attn(lam_rows, sub_g[:, None], q, k, vt, kc, vc, lam_init)
    return _outproj(h, o, mods, w_out.astype(BF16))


def kernel(x, c, ctx, c_ctx, l0_ada_w, l0_ada_b, l0_norm1, l0_norm2, l0_sc_in, l0_sc_conv, l0_sc_out, l0_ffn_up, l0_ffn_conv_w, l0_ffn_conv_b, l0_ffn_down, l1_ada_w, l1_ada_b, l1_norm1, l1_norm2, l1_wa_qkv, l1_wa_qnorm, l1_wa_knorm, l1_wa_sink, l1_wa_out, l1_ffn_up, l1_ffn_conv_w, l1_ffn_conv_b, l1_ffn_down, l2_ada_w, l2_ada_b, l2_norm1, l2_norm2, l2_da_qkv, l2_da_qnorm, l2_da_knorm, l2_da_lq1, l2_da_lk1, l2_da_lq2, l2_da_lk2, l2_da_subln, l2_da_out, l2_ffn_up, l2_ffn_conv_w, l2_ffn_conv_b, l2_ffn_down, l3_ada_w, l3_ada_b, l3_norm1, l3_norm2, l3_sc_in, l3_sc_conv, l3_sc_out, l3_ffn_up, l3_ffn_conv_w, l3_ffn_conv_b, l3_ffn_down):
    commons = [(l0_ada_w, l0_ada_b, l0_norm1, l0_norm2),
               (l1_ada_w, l1_ada_b, l1_norm1, l1_norm2),
               (l2_ada_w, l2_ada_b, l2_norm1, l2_norm2),
               (l3_ada_w, l3_ada_b, l3_norm1, l3_norm2)]
    mixers = [(l0_sc_in, l0_sc_conv, l0_sc_out),
              (l1_wa_qkv, l1_wa_qnorm, l1_wa_knorm, l1_wa_sink, l1_wa_out),
              (l2_da_qkv, l2_da_qnorm, l2_da_knorm, l2_da_lq1, l2_da_lk1, l2_da_lq2, l2_da_lk2,
               l2_da_subln, l2_da_out),
              (l3_sc_in, l3_sc_conv, l3_sc_out)]
    ffns = [(l0_ffn_up, l0_ffn_conv_w, l0_ffn_conv_b, l0_ffn_down),
            (l1_ffn_up, l1_ffn_conv_w, l1_ffn_conv_b, l1_ffn_down),
            (l2_ffn_up, l2_ffn_conv_w, l2_ffn_conv_b, l2_ffn_down),
            (l3_ffn_up, l3_ffn_conv_w, l3_ffn_conv_b, l3_ffn_down)]
    depth = len(commons)
    bsz, seq, d = x.shape
    rope_tabs = _rope_tables(seq)

    n_rows = -(-(bsz + 1) // SUBLANES) * SUBLANES
    cvec = jnp.zeros((n_rows, d), F32).at[:bsz].set(c).at[bsz].set(c_ctx)

    h, hc = x, ctx
    for i in range(depth):
        kind = i % N_MIXERS
        ada_w, ada_b, g1, g2 = commons[i]
        ctx_after = any(j % N_MIXERS != 0 for j in range(i + 1, depth))
        mods_all = _ada(cvec, ada_w, ada_b).reshape(n_rows, 6, d)
        mods, mods_c = mods_all[:bsz], mods_all[bsz:bsz + 1]
        g1, g2 = g1[None, :], g2[None, :]

        if kind == 0:
            w_in, w_conv, w_out = mixers[i]
            w_in, w_out = w_in.astype(BF16), w_out.astype(BF16)
            h = _conv_mixer(h, mods, g1, w_in, w_conv, w_out)
            if ctx_after:
                hc = _conv_mixer(hc, mods_c, g1, w_in, w_conv, w_out)
        elif kind == 1:
            h, hc_new = _wa_layer(h, hc, mods, mods_c, g1, *mixers[i], rope_tabs)
            if ctx_after:
                hc = hc_new
        else:
            h = _da_layer(h, hc, mods, mods_c, g1, *mixers[i], i, rope_tabs)
            assert not ctx_after

        w_up, conv_w, conv_b, w_down = ffns[i]
        w_up, w_down = w_up.astype(BF16), w_down.astype(BF16)
        conv_b = conv_b[None, :]
        h = _ffn(h, mods, g2, w_up, conv_w, conv_b, w_down)
        if ctx_after:
            hc = _ffn(hc, mods_c, g2, w_up, conv_w, conv_b, w_down)
    return h
```

```python
import functools
import math

import jax
import jax.numpy as jnp
from jax import lax
from jax.experimental import pallas as pl
from jax.experimental.pallas import tpu as pltpu

F32 = jnp.float32
BF16 = jnp.bfloat16

LANES = 128
SUBLANES = 8
MXU_DIM = 256
VMEM_LIMIT_BYTES = 56 * 1024 * 1024

GRID_W = 64
HEAD_DIM = 64
WA_GROUP = 4
WINDOW = 128
CONV_W = 3
N_MIXERS = 3
ROPE_BASE = 10000.0
EPS = 1e-6
NEG = -1e30

NT_DIMS = (((1,), (1,)), ((), ()))
LOG2E = 1.4426950408889634


def _cparams(sem):
    return pltpu.CompilerParams(dimension_semantics=sem, vmem_limit_bytes=VMEM_LIMIT_BYTES)


def _sigmoid(x):
    return 1.0 / (1.0 + jnp.exp(-x))


def _norm_mod(x, g, shift, scale):
    ms = jnp.mean(x * x, axis=-1, keepdims=True)
    y = x * lax.rsqrt(ms + EPS) * g
    return y * (1.0 + scale) + shift


def _row_tile(s):
    for tm in (512, 256, 128):
        if s % tm == 0:
            return tm
    raise ValueError(f"sequence length {s} must be a multiple of 128")


def _halo_specs(tm, d, s):
    per = tm // SUBLANES
    last = s // SUBLANES - 1
    cur = pl.BlockSpec((1, tm, d), lambda b, t: (b, t, 0))
    prev = pl.BlockSpec((1, SUBLANES, d), lambda b, t: (b, jnp.maximum(t * per - 1, 0), 0))
    nxt = pl.BlockSpec((1, SUBLANES, d), lambda b, t: (b, jnp.minimum((t + 1) * per, last), 0))
    return cur, prev, nxt


def _mod_spec(mods, d):
    if mods.shape[0] == 1:
        return pl.BlockSpec((1, 6, d), lambda b, t: (0, 0, 0))
    return pl.BlockSpec((1, 6, d), lambda b, t: (b, 0, 0))


def _const_spec(shape):
    zeros = (0,) * len(shape)
    return pl.BlockSpec(shape, lambda *_: zeros, pipeline_mode=pl.Buffered(1))


def _ada_kernel(c_ref, w_ref, b_ref, o_ref):
    cv = c_ref[...]
    act = cv * _sigmoid(cv)
    o_ref[...] = jnp.dot(act, w_ref[...], preferred_element_type=F32,
                         precision=lax.Precision.HIGHEST) + b_ref[...]


def _ada(cvec, ada_w, ada_b):
    r, d = cvec.shape
    n = ada_w.shape[1]
    tn = n // 4
    return pl.pallas_call(
        _ada_kernel,
        grid=(n // tn,),
        in_specs=[pl.BlockSpec((r, d), lambda i: (0, 0)),
                  pl.BlockSpec((d, tn), lambda i: (0, i)),
                  pl.BlockSpec((1, tn), lambda i: (0, i))],
        out_specs=pl.BlockSpec((r, tn), lambda i: (0, i)),
        out_shape=jax.ShapeDtypeStruct((r, n), F32),
        compiler_params=_cparams(("arbitrary",)),
        name="ada_mod",
    )(cvec, ada_w, ada_b.reshape(1, n))


def _shift_conv(scr, pieces, tm, w0, w1, w2):
    prev, cur, nxt = pieces
    scr[0:SUBLANES, :] = prev
    scr[SUBLANES:SUBLANES + tm, :] = cur
    scr[SUBLANES + tm:, :] = nxt
    return (w0 * scr[SUBLANES - 1:SUBLANES - 1 + tm, :] + w1 * cur
            + w2 * scr[SUBLANES + 1:SUBLANES + 1 + tm, :])


def _conv_mixer_kernel(h_ref, hp_ref, hn_ref, mod_ref, g_ref, win_ref, wc_ref, wout_ref,
                       o_ref, scr, *, n_tiles):
    t = pl.program_id(1)
    tm, d = h_ref.shape[1], h_ref.shape[2]
    hcur = h_ref[0]
    shift, scale, gate = mod_ref[0, 0:1, :], mod_ref[0, 1:2, :], mod_ref[0, 2:3, :]
    x_ext = jnp.concatenate([hcur, hp_ref[0], hn_ref[0]], axis=0)
    xn = _norm_mod(x_ext, g_ref[...], shift, scale).astype(BF16)
    b_gate = jnp.dot(xn[:tm], win_ref[:, 0:d], preferred_element_type=F32)
    c_gate = jnp.dot(xn, win_ref[:, d:2 * d], preferred_element_type=F32)
    hh = jnp.dot(xn, win_ref[:, 2 * d:3 * d], preferred_element_type=F32)
    ch = c_gate * hh
    prev = jnp.where(t > 0, ch[tm:tm + SUBLANES], 0.0)
    nxt = jnp.where(t < n_tiles - 1, ch[tm + SUBLANES:], 0.0)
    conv = _shift_conv(scr, (prev, ch[:tm], nxt), tm, wc_ref[0:1, :], wc_ref[1:2, :], wc_ref[2:3, :])
    z = (b_gate * conv).astype(BF16)
    y = jnp.dot(z, wout_ref[...], preferred_element_type=F32)
    o_ref[0] = hcur + gate * y


def _conv_mixer(h, mods, g1, w_in, w_conv, w_out):
    b, s, d = h.shape
    tm = _row_tile(s)
    n_tiles = s // tm
    cur, prev, nxt = _halo_specs(tm, d, s)
    return pl.pallas_call(
        functools.partial(_conv_mixer_kernel, n_tiles=n_tiles),
        grid=(b, n_tiles),
        in_specs=[cur, prev, nxt, _mod_spec(mods, d), _const_spec((1, d)),
                  _const_spec((d, 3 * d)), _const_spec((CONV_W, d)), _const_spec((d, d))],
        out_specs=pl.BlockSpec((1, tm, d), lambda bi, t: (bi, t, 0)),
        out_shape=jax.ShapeDtypeStruct((b, s, d), F32),
        scratch_shapes=[pltpu.VMEM((tm + 2 * SUBLANES, d), F32)],
        compiler_params=_cparams(("parallel", "parallel")),
        name="conv_mixer",
    )(h, h, h, mods, g1, w_in, w_conv, w_out)


def _ffn_chunk(dff):
    best = LANES
    for fc in range(LANES, 1408 + 1, LANES):
        if dff % fc == 0:
            best = fc
    return best


def _ffn_kernel(h_ref, hp_ref, hn_ref, mod_ref, g_ref, wup_ref, cw_ref, cb_ref, wdn_ref,
                o_ref, scr, *, n_tiles, fc):
    t = pl.program_id(1)
    tm, d = h_ref.shape[1], h_ref.shape[2]
    dff = wdn_ref.shape[0]
    hcur = h_ref[0]
    shift, scale, gate = mod_ref[0, 3:4, :], mod_ref[0, 4:5, :], mod_ref[0, 5:6, :]
    x_ext = jnp.concatenate([hcur, hp_ref[0], hn_ref[0]], axis=0)
    xn = _norm_mod(x_ext, g_ref[...], shift, scale).astype(BF16)
    acc = jnp.zeros((tm, d), F32)
    for c0 in range(0, dff, fc):
        a = jnp.dot(xn[:tm], wup_ref[:, c0:c0 + fc], preferred_element_type=F32)
        gg = jnp.dot(xn, wup_ref[:, dff + c0:dff + c0 + fc], preferred_element_type=F32)
        prev = jnp.where(t > 0, gg[tm:tm + SUBLANES], 0.0)
        nxt = jnp.where(t < n_tiles - 1, gg[tm + SUBLANES:], 0.0)
        gc = _shift_conv(scr, (prev, gg[:tm], nxt), tm, cw_ref[0:1, c0:c0 + fc],
                         cw_ref[1:2, c0:c0 + fc], cw_ref[2:3, c0:c0 + fc]) + cb_ref[:, c0:c0 + fc]
        u = (gc * _sigmoid(gc) * a).astype(BF16)
        acc = acc + jnp.dot(u, wdn_ref[c0:c0 + fc, :], preferred_element_type=F32)
    o_ref[0] = hcur + gate * acc


def _ffn(h, mods, g2, w_up, conv_w, conv_b, w_down):
    b, s, d = h.shape
    dff = w_down.shape[0]
    tm = _row_tile(s)
    n_tiles = s // tm
    fc = _ffn_chunk(dff)
    cur, prev, nxt = _halo_specs(tm, d, s)
    return pl.pallas_call(
        functools.partial(_ffn_kernel, n_tiles=n_tiles, fc=fc),
        grid=(b, n_tiles),
        in_specs=[cur, prev, nxt, _mod_spec(mods, d), _const_spec((1, d)),
                  _const_spec((d, 2 * dff)), _const_spec((CONV_W, dff)), _const_spec((1, dff)),
                  _const_spec((dff, d))],
        out_specs=pl.BlockSpec((1, tm, d), lambda bi, t: (bi, t, 0)),
        out_shape=jax.ShapeDtypeStruct((b, s, d), F32),
        scratch_shapes=[pltpu.VMEM((tm + 2 * SUBLANES, fc), F32)],
        compiler_params=_cparams(("parallel", "parallel")),
        name="conv_ffn",
    )(h, h, h, mods, g2, w_up, conv_w, conv_b, w_down)


def _head_sum_matrix():
    i = jnp.arange(MXU_DIM) // HEAD_DIM
    return (i[:, None] == i[None, :]).astype(BF16)


def _rope_tables(n_tokens):
    rows = n_tokens // GRID_W
    row = jnp.repeat(jnp.arange(rows, dtype=F32), GRID_W)
    col = jnp.tile(jnp.arange(GRID_W, dtype=F32), rows)
    m = HEAD_DIM // 4
    inv_freq = ROPE_BASE ** (-jnp.arange(m, dtype=F32) / m)
    ang = jnp.stack([row, col], axis=-1)[:, :, None] * inv_freq
    cos, sin = jnp.cos(ang), jnp.sin(ang)
    c64 = jnp.concatenate([cos[:, 0], cos[:, 0], cos[:, 1], cos[:, 1]], axis=-1)
    s64 = jnp.concatenate([-sin[:, 0], sin[:, 0], -sin[:, 1], sin[:, 1]], axis=-1)
    reps = LANES // HEAD_DIM
    return jnp.tile(c64, (1, reps)), jnp.tile(s64, (1, reps))


def _proj_kernel(*refs, n_norm, widths, transposed, rope):
    h_ref, mod_ref, g_ref, w_ref, bm_ref, gain_ref = refs[:6]
    pos = 6
    if rope:
        cos_ref, sin_ref = refs[6:8]
        pos = 8
    out_refs = refs[pos:]
    shift, scale = mod_ref[0, 0:1, :], mod_ref[0, 1:2, :]
    xn = _norm_mod(h_ref[0], g_ref[...], shift, scale).astype(BF16)
    u = jnp.dot(xn, w_ref[...], preferred_element_type=F32)

    starts = [sum(widths[:i]) for i in range(len(widths))]

    def store(c0, val):
        for ref, st, wd, tr in zip(out_refs, starts, widths, transposed):
            if st <= c0 < st + wd and tr:
                ref[0, c0 - st:c0 - st + val.shape[1], :] = val.T.astype(BF16)
            elif st <= c0 < st + wd:
                ref[0, :, c0 - st:c0 - st + val.shape[1]] = val.astype(BF16)

    if rope:
        cos_t, sin_t = cos_ref[...], sin_ref[...]
        lane = lax.broadcasted_iota(jnp.int32, (1, LANES), 1)
        upper = (lane & (HEAD_DIM // 4)) != 0
    for c0 in range(0, n_norm, MXU_DIM):
        xb = u[:, c0:c0 + MXU_DIM]
        ss = jnp.dot((xb * xb).astype(BF16), bm_ref[...], preferred_element_type=F32)
        xb = xb * lax.rsqrt(ss * (1.0 / HEAD_DIM) + EPS) * gain_ref[:, c0:c0 + MXU_DIM]
        for c1 in range(0, MXU_DIM, LANES):
            xs = xb[:, c1:c1 + LANES]
            if rope:
                partner = jnp.where(upper, pltpu.roll(xs, HEAD_DIM // 4, 1),
                                    pltpu.roll(xs, LANES - HEAD_DIM // 4, 1))
                xs = xs * cos_t + partner * sin_t
            store(c0 + c1, xs)
    for c0 in range(n_norm, sum(widths), LANES):
        store(c0, u[:, c0:c0 + LANES])


def _proj(h, mods, g1, w, gains, widths, transposed, n_norm, rope_tabs):
    b, s, d = h.shape
    n = w.shape[1]
    tm = _row_tile(s)
    rope = rope_tabs is not None
    in_specs = [pl.BlockSpec((1, tm, d), lambda bi, t: (bi, t, 0)), _mod_spec(mods, d),
                _const_spec((1, d)), _const_spec((d, n)), _const_spec((MXU_DIM, MXU_DIM)),
                _const_spec((1, n_norm))]
    args = [h, mods, g1, w, _head_sum_matrix(), gains]
    if rope:
        in_specs += [pl.BlockSpec((tm, LANES), lambda bi, t: (t, 0))] * 2
        args += list(rope_tabs)
    out_specs = [pl.BlockSpec((1, wd, tm), lambda bi, t: (bi, 0, t)) if tr
                 else pl.BlockSpec((1, tm, wd), lambda bi, t: (bi, t, 0)) for wd, tr in zip(widths, transposed)]
    out_shape = [jax.ShapeDtypeStruct((b, wd, s) if tr else (b, s, wd), BF16)
                 for wd, tr in zip(widths, transposed)]
    return pl.pallas_call(
        functools.partial(_proj_kernel, n_norm=n_norm, widths=tuple(widths),
                          transposed=tuple(transposed), rope=rope),
        grid=(b, s // tm),
        in_specs=in_specs,
        out_specs=out_specs,
        out_shape=out_shape,
        compiler_params=_cparams(("parallel", "parallel")),
        name="qkv_proj_rope" if rope else "qkv_proj",
    )(*args)


def _outproj_kernel(h_ref, o_ref, mod_ref, w_ref, out_ref):
    gate = mod_ref[0, 2:3, :]
    y = jnp.dot(o_ref[0], w_ref[...], preferred_element_type=F32)
    out_ref[0] = h_ref[0] + gate * y


def _outproj(h, o, mods, w_out):
    b, s, d = h.shape
    k = o.shape[2]
    tm = _row_tile(s)
    return pl.pallas_call(
        _outproj_kernel,
        grid=(b, s // tm),
        in_specs=[pl.BlockSpec((1, tm, d), lambda bi, t: (bi, t, 0)),
                  pl.BlockSpec((1, tm, k), lambda bi, t: (bi, t, 0)),
                  _mod_spec(mods, d), _const_spec((k, d))],
        out_specs=pl.BlockSpec((1, tm, d), lambda bi, t: (bi, t, 0)),
        out_shape=jax.ShapeDtypeStruct((b, s, d), F32),
        compiler_params=_cparams(("parallel", "parallel")),
        name="out_proj_residual",
    )(h, o, mods, w_out)


def _lane_block_max(x, acc=None):
    for c0 in range(0, x.shape[1], LANES):
        blk = x[:, c0:c0 + LANES]
        acc = blk if acc is None else jnp.maximum(acc, blk)
    return acc


def _wa_attn_kernel(*refs, n_blocks, band):
    if band:
        sink_ref, q_ref, kp_ref, kc_ref, kn_ref, vp_ref, vc_ref, vn_ref, kx_ref, vx_ref, o_ref = refs
    else:
        sink_ref, q_ref, kx_ref, vx_ref, o_ref = refs
    j = pl.program_id(1)
    tq = q_ref.shape[1]
    n_kv = vx_ref.shape[2] // LANES
    rows = WA_GROUP * tq
    lane = lax.broadcasted_iota(jnp.int32, (1, LANES), 1)
    lo = lane < HEAD_DIM
    lo_bf = jnp.where(lo, 1.0, 0.0).astype(BF16)
    hi_bf = jnp.where(lo, 0.0, 1.0).astype(BF16)
    if band:
        n_band = tq + 2 * WINDOW
        r = lax.broadcasted_iota(jnp.int32, (rows, n_band), 0) & (tq - 1)
        c = lax.broadcasted_iota(jnp.int32, (rows, n_band), 1)
        in_window = (c - r).astype(jnp.uint32) <= jnp.uint32(2 * WINDOW)
        c_lo = jnp.where(j > 0, 0, WINDOW)
        c_hi = jnp.where(j < n_blocks - 1, n_band, WINDOW + tq)
        ok = in_window & (c >= c_lo) & (c < c_hi)
    q = q_ref[0]
    for pair in range(n_kv // 2):
        probs, vals, stats = [], [], []
        for kv in (2 * pair, 2 * pair + 1):
            cs = slice(kv * LANES, (kv + 1) * LANES)
            qb0 = q[:, (2 * kv) * LANES:(2 * kv + 1) * LANES]
            qb1 = q[:, (2 * kv + 1) * LANES:(2 * kv + 2) * LANES]
            q4 = jnp.concatenate([qb0 * lo_bf, qb0 * hi_bf, qb1 * lo_bf, qb1 * hi_bf], axis=0)
            sk = jnp.concatenate([jnp.full((tq, 1), sink_ref[WA_GROUP * kv + g] * LOG2E, F32)
                                  for g in range(WA_GROUP)], axis=0)
            s_x = jnp.dot(q4, kx_ref[0, cs, :], preferred_element_type=F32)
            blk_max = _lane_block_max(s_x)
            v_rows = [vx_ref[0, :, cs]]
            if band:
                kb = jnp.concatenate([kp_ref[0, cs, :], kc_ref[0, cs, :], kn_ref[0, cs, :]], axis=1)
                s_b = jnp.dot(q4, kb, preferred_element_type=F32)
                s_b = jnp.where(ok, s_b, NEG)
                blk_max = _lane_block_max(s_b, blk_max)
                v_rows = [vp_ref[0, :, cs], vc_ref[0, :, cs], vn_ref[0, :, cs]] + v_rows
            m = jnp.maximum(jnp.max(blk_max, axis=1, keepdims=True), sk)
            p = [jnp.exp2((s_x - m).astype(BF16))]
            if band:
                p = [jnp.exp2((s_b - m).astype(BF16))] + p
            v_all = jnp.concatenate(v_rows, axis=0)
            probs.append(jnp.concatenate(p, axis=1))
            vals.append(jnp.where(lo, v_all, jnp.ones_like(v_all)))
            stats.append((m, sk))
        zeros = jnp.zeros_like(vals[0])
        v_diag = jnp.concatenate([jnp.concatenate([vals[0], zeros], axis=1),
                                  jnp.concatenate([zeros, vals[1]], axis=1)], axis=0)
        o2 = jnp.dot(jnp.concatenate(probs, axis=1), v_diag, preferred_element_type=F32)
        for i, kv in enumerate((2 * pair, 2 * pair + 1)):
            oa = o2[:, i * LANES:(i + 1) * LANES]
            m, sk = stats[i]
            on = oa / (oa[:, HEAD_DIM:HEAD_DIM + 1] + jnp.exp2(sk - m))
            for half in range(2):
                even = on[(2 * half) * tq:(2 * half + 1) * tq]
                odd = on[(2 * half + 1) * tq:(2 * half + 2) * tq]
                blk = jnp.where(lo, even, pltpu.roll(odd, HEAD_DIM, 1))
                o_ref[0, :, (2 * kv + half) * LANES:(2 * kv + half + 1) * LANES] = blk.astype(BF16)


WA_TQ = 128


def _wa_attn(sink, q, kt, v, kxt, vx, band):
    b, s, d = q.shape
    nk = vx.shape[2]
    n_ctx = vx.shape[1]
    smem = pl.BlockSpec(memory_space=pltpu.SMEM)
    if band:
        tq = WA_TQ if s % WA_TQ == 0 else WINDOW
        nb = s // tq
        per = tq // WINDOW
        last = s // WINDOW - 1
        before = lambda j: jnp.maximum(j * per - 1, 0)
        after = lambda j: jnp.minimum((j + 1) * per, last)
        k_specs = [pl.BlockSpec((1, nk, WINDOW), lambda bi, j: (bi, 0, before(j))),
                   pl.BlockSpec((1, nk, tq), lambda bi, j: (bi, 0, j)),
                   pl.BlockSpec((1, nk, WINDOW), lambda bi, j: (bi, 0, after(j)))]
        v_specs = [pl.BlockSpec((1, WINDOW, nk), lambda bi, j: (bi, before(j), 0)),
                   pl.BlockSpec((1, tq, nk), lambda bi, j: (bi, j, 0)),
                   pl.BlockSpec((1, WINDOW, nk), lambda bi, j: (bi, after(j), 0))]
        in_specs = [smem, pl.BlockSpec((1, tq, d), lambda bi, j: (bi, j, 0))] + k_specs + v_specs
        args = [sink, q, kt, kt, kt, v, v, v]
    else:
        tq = _row_tile(s)
        nb = s // tq
        in_specs = [smem, pl.BlockSpec((1, tq, d), lambda bi, j: (bi, j, 0))]
        args = [sink, q]
    in_specs += [pl.BlockSpec((1, nk, n_ctx), lambda bi, j: (bi, 0, 0)),
                 pl.BlockSpec((1, n_ctx, nk), lambda bi, j: (bi, 0, 0))]
    args += [kxt, vx]
    return pl.pallas_call(
        functools.partial(_wa_attn_kernel, n_blocks=nb, band=band),
        grid=(b, nb),
        in_specs=in_specs,
        out_specs=pl.BlockSpec((1, tq, d), lambda bi, j: (bi, j, 0)),
        out_shape=jax.ShapeDtypeStruct((b, s, d), BF16),
        compiler_params=_cparams(("parallel", "parallel")),
        name="window_gqa" if band else "ctx_gqa",
    )(*args)


DA_TK = 512
DA_TQ = 512


def _da_attn_kernel(lam_ref, subg_ref, q_ref, kt_ref, v_ref, kxt_ref, vx_ref, o_ref, s_scr, *, lam_init):
    tq = q_ref.shape[1]
    s_len = v_ref.shape[1]
    n_ctx = vx_ref.shape[1]
    lane = lax.broadcasted_iota(jnp.int32, (1, LANES), 1)
    lo = lane < HEAD_DIM
    lo_bf = jnp.where(lo, 1.0, 0.0).astype(BF16)
    hi_bf = jnp.where(lo, 0.0, 1.0).astype(BF16)
    q = q_ref[0]
    qq = jnp.concatenate([q * lo_bf, q * hi_bf], axis=0)

    chunks = [(kt_ref, v_ref, r0, min(DA_TK, s_len - r0)) for r0 in range(0, s_len, DA_TK)]
    chunks.append((kxt_ref, vx_ref, 0, n_ctx))

    blk_max = None
    base = 0
    for kref, _, r0, n in chunks:
        s = jnp.dot(qq, kref[0, :, r0:r0 + n], preferred_element_type=F32)
        s_scr[:, base:base + n] = s
        blk_max = _lane_block_max(s, blk_max)
        base += n
    m = jnp.max(blk_max, axis=1, keepdims=True)
    acc = [jnp.zeros((tq, 2 * LANES), F32) for _ in range(2)]
    base = 0
    for _, vref, r0, n in chunks:
        v_aug = jnp.concatenate([vref[0, r0:r0 + n, :], jnp.ones((n, LANES), BF16)], axis=1)
        for comp in range(2):
            rs = slice(comp * tq, (comp + 1) * tq)
            p = jnp.exp2((s_scr[rs, base:base + n] - m[rs]).astype(BF16))
            acc[comp] = acc[comp] + jnp.dot(p, v_aug, preferred_element_type=F32)
        base += n
    on = [a[:, :LANES] / a[:, LANES:] for a in acc]
    lam = (jnp.exp(jnp.sum(lam_ref[0:1, :] * lam_ref[1:2, :], axis=-1, keepdims=True))
           - jnp.exp(jnp.sum(lam_ref[2:3, :] * lam_ref[3:4, :], axis=-1, keepdims=True)) + lam_init)
    o = on[0] - lam * on[1]
    o = o * lax.rsqrt(jnp.mean(o * o, axis=-1, keepdims=True) + EPS) * subg_ref[...]
    o_ref[0] = (o * (1.0 - lam_init)).astype(BF16)


def _da_attn(lam_rows, sub_g, q, kt, v, kxt, vx, lam_init):
    b, s, d = q.shape
    n_ctx = vx.shape[1]
    n_heads = d // LANES
    tq = DA_TQ if s % DA_TQ == 0 else _row_tile(s)
    head_rows = lambda n: pl.BlockSpec((1, n, LANES), lambda bi, hi, qi: (bi, 0, hi))
    head_cols = lambda n: pl.BlockSpec((1, LANES, n), lambda bi, hi, qi: (bi, hi, 0))
    return pl.pallas_call(
        functools.partial(_da_attn_kernel, lam_init=lam_init),
        grid=(b, n_heads, s // tq),
        in_specs=[pl.BlockSpec((SUBLANES, LANES), lambda bi, hi, qi: (0, 0)),
                  pl.BlockSpec((1, LANES), lambda bi, hi, qi: (0, 0)),
                  pl.BlockSpec((1, tq, LANES), lambda bi, hi, qi: (bi, qi, hi)),
                  head_cols(s), head_rows(s), head_cols(n_ctx), head_rows(n_ctx)],
        out_specs=pl.BlockSpec((1, tq, LANES), lambda bi, hi, qi: (bi, qi, hi)),
        out_shape=jax.ShapeDtypeStruct((b, s, d), BF16),
        scratch_shapes=[pltpu.VMEM((2 * tq, s + n_ctx), F32)],
        compiler_params=_cparams(("parallel", "parallel", "arbitrary")),
        name="diff_attn",
    )(lam_rows, sub_g, q, kt, v, kxt, vx)


def _widen_heads(w, n_heads, dup):
    d = w.shape[0]
    w = w.reshape(d, n_heads, HEAD_DIM)
    other = w if dup else jnp.zeros_like(w)
    return jnp.concatenate([w, other], axis=-1).reshape(d, n_heads * LANES)


def _wa_layer(h, hc, mods, mods_c, g1, w_qkv, q_g, k_g, sink, w_out, rope_tabs):
    d = h.shape[2]
    n_q = d // HEAD_DIM
    n_kv = n_q // WA_GROUP
    wq = w_qkv[:, :d]
    wk = _widen_heads(w_qkv[:, d:d + n_kv * HEAD_DIM], n_kv, dup=True)
    wv = _widen_heads(w_qkv[:, d + n_kv * HEAD_DIM:], n_kv, dup=False)
    w = jnp.concatenate([wq, wk, wv], axis=1).astype(BF16)
    nk = n_kv * LANES
    gains = jnp.concatenate([jnp.tile(q_g, n_q) * (HEAD_DIM ** -0.5 * LOG2E),
                             jnp.tile(k_g, 2 * n_kv)])[None, :]
    widths = (d, nk, nk)
    tr = (False, True, False)
    q, k, v = _proj(h, mods, g1, w, gains, widths, tr, d + nk, rope_tabs)
    qc, kc, vc = _proj(hc, mods_c, g1, w, gains, widths, tr, d + nk, None)
    w_out = w_out.astype(BF16)
    o = _wa_attn(sink, q, k, v, kc, vc, band=True)
    oc = _wa_attn(sink, qc, None, None, kc, vc, band=False)
    return _outproj(h, o, mods, w_out), _outproj(hc, oc, mods_c, w_out)


def _da_layer(h, hc, mods, mods_c, g1, w_qkv, q_g, k_g, lq1, lk1, lq2, lk2, sub_g, w_out,
              layer_idx, rope_tabs):
    d = h.shape[2]
    n_qk = 2 * d // HEAD_DIM
    w = w_qkv.astype(BF16)
    gains = jnp.concatenate([jnp.tile(q_g, n_qk // 2) * (HEAD_DIM ** -0.5 * LOG2E),
                             jnp.tile(k_g, n_qk // 2)])[None, :]
    widths = (d, d, d)
    tr = (False, True, False)
    q, k, v = _proj(h, mods, g1, w, gains, widths, tr, 2 * d, rope_tabs)
    _, kc, vc = _proj(hc, mods_c, g1, w, gains, widths, tr, 2 * d, None)
    lam_init = 0.8 - 0.6 * math.exp(-0.3 * layer_idx)
    lam_rows = jnp.zeros((SUBLANES, LANES), F32).at[:4, :HEAD_DIM].set(jnp.stack([lq1, lk1, lq2, lk2]))
    o = _da_attn(lam_rows, sub_g[None, :], q, k, v, kc, vc, lam_init)
    return _outproj(h, o, mods, w_out.astype(BF16))


def kernel(x, c, ctx, c_ctx, l0_ada_w, l0_ada_b, l0_norm1, l0_norm2, l0_sc_in, l0_sc_conv, l0_sc_out, l0_ffn_up, l0_ffn_conv_w, l0_ffn_conv_b, l0_ffn_down, l1_ada_w, l1_ada_b, l1_norm1, l1_norm2, l1_wa_qkv, l1_wa_qnorm, l1_wa_knorm, l1_wa_sink, l1_wa_out, l1_ffn_up, l1_ffn_conv_w, l1_ffn_conv_b, l1_ffn_down, l2_ada_w, l2_ada_b, l2_norm1, l2_norm2, l2_da_qkv, l2_da_qnorm, l2_da_knorm, l2_da_lq1, l2_da_lk1, l2_da_lq2, l2_da_lk2, l2_da_subln, l2_da_out, l2_ffn_up, l2_ffn_conv_w, l2_ffn_conv_b, l2_ffn_down, l3_ada_w, l3_ada_b, l3_norm1, l3_norm2, l3_sc_in, l3_sc_conv, l3_sc_out, l3_ffn_up, l3_ffn_conv_w, l3_ffn_conv_b, l3_ffn_down):
    commons = [(l0_ada_w, l0_ada_b, l0_norm1, l0_norm2),
               (l1_ada_w, l1_ada_b, l1_norm1, l1_norm2),
               (l2_ada_w, l2_ada_b, l2_norm1, l2_norm2),
               (l3_ada_w, l3_ada_b, l3_norm1, l3_norm2)]
    mixers = [(l0_sc_in, l0_sc_conv, l0_sc_out),
              (l1_wa_qkv, l1_wa_qnorm, l1_wa_knorm, l1_wa_sink, l1_wa_out),
              (l2_da_qkv, l2_da_qnorm, l2_da_knorm, l2_da_lq1, l2_da_lk1, l2_da_lq2, l2_da_lk2,
               l2_da_subln, l2_da_out),
              (l3_sc_in, l3_sc_conv, l3_sc_out)]
    ffns = [(l0_ffn_up, l0_ffn_conv_w, l0_ffn_conv_b, l0_ffn_down),
            (l1_ffn_up, l1_ffn_conv_w, l1_ffn_conv_b, l1_ffn_down),
            (l2_ffn_up, l2_ffn_conv_w, l2_ffn_conv_b, l2_ffn_down),
            (l3_ffn_up, l3_ffn_conv_w, l3_ffn_conv_b, l3_ffn_down)]
    depth = len(commons)
    bsz, seq, d = x.shape
    rope_tabs = _rope_tables(seq)

    n_rows = -(-(bsz + 1) // SUBLANES) * SUBLANES
    cvec = jnp.zeros((n_rows, d), F32).at[:bsz].set(c).at[bsz].set(c_ctx)

    h, hc = x, ctx
    for i in range(depth):
        kind = i % N_MIXERS
        ada_w, ada_b, g1, g2 = commons[i]
        ctx_after = any(j % N_MIXERS != 0 for j in range(i + 1, depth))
        mods_all = _ada(cvec, ada_w, ada_b).reshape(n_rows, 6, d)
        mods, mods_c = mods_all[:bsz], mods_all[bsz:bsz + 1]
        g1, g2 = g1[None, :], g2[None, :]

        if kind == 0:
            w_in, w_conv, w_out = mixers[i]
            w_in, w_out = w_in.astype(BF16), w_out.astype(BF16)
            h = _conv_mixer(h, mods, g1, w_in, w_conv, w_out)
            if ctx_after:
                hc = _conv_mixer(hc, mods_c, g1, w_in, w_conv, w_out)
        elif kind == 1:
            h, hc_new = _wa_layer(h, hc, mods, mods_c, g1, *mixers[i], rope_tabs)
            if ctx_after:
                hc = hc_new
        else:
            h = _da_layer(h, hc, mods, mods_c, g1, *mixers[i], i, rope_tabs)
            assert not ctx_after

        w_up, conv_w, conv_b, w_down = ffns[i]
        w_up, w_down = w_up.astype(BF16), w_down.astype(BF16)
        conv_b = conv_b[None, :]
        h = _ffn(h, mods, g2, w_up, conv_w, conv_b, w_down)
        if ctx_after:
            hc = _ffn(hc, mods_c, g2, w_up, conv_w, conv_b, w_down)
    return h
```

```python
import functools
import math

import jax
import jax.numpy as jnp
from jax import lax
from jax.experimental import pallas as pl
from jax.experimental.pallas import tpu as pltpu

F32 = jnp.float32
BF16 = jnp.bfloat16

LANES = 128
SUBLANES = 8
BF16_ROWS = 16
MXU_DIM = 256
VMEM_LIMIT_BYTES = 56 * 1024 * 1024

GRID_W = 64
HEAD_DIM = 64
WA_GROUP = 4
WINDOW = 128
CONV_W = 3
N_MIXERS = 3
ROPE_BASE = 10000.0
EPS = 1e-6
NEG = -1e30

NT_DIMS = (((1,), (1,)), ((), ()))
LOG2E = 1.4426950408889634


def _cparams(sem):
    return pltpu.CompilerParams(dimension_semantics=sem, vmem_limit_bytes=VMEM_LIMIT_BYTES)


def _sigmoid(x):
    return 1.0 / (1.0 + jnp.exp(-x))


def _norm_mod(x, g, shift, scale):
    ms = jnp.mean(x * x, axis=-1, keepdims=True)
    y = x * lax.rsqrt(ms + EPS) * g
    return y * (1.0 + scale) + shift


def _row_tile(s):
    for tm in (512, 256, 128):
        if s % tm == 0:
            return tm
    raise ValueError(f"sequence length {s} must be a multiple of 128")


def _halo_specs(tm, d, s):
    per = tm // SUBLANES
    last = s // SUBLANES - 1
    cur = pl.BlockSpec((1, tm, d), lambda b, t: (b, t, 0))
    prev = pl.BlockSpec((1, SUBLANES, d), lambda b, t: (b, jnp.maximum(t * per - 1, 0), 0))
    nxt = pl.BlockSpec((1, SUBLANES, d), lambda b, t: (b, jnp.minimum((t + 1) * per, last), 0))
    return cur, prev, nxt


def _mod_spec(mods, d):
    if mods.shape[0] == 1:
        return pl.BlockSpec((1, 6, d), lambda b, t: (0, 0, 0))
    return pl.BlockSpec((1, 6, d), lambda b, t: (b, 0, 0))


def _const_spec(shape):
    zeros = (0,) * len(shape)
    return pl.BlockSpec(shape, lambda *_: zeros, pipeline_mode=pl.Buffered(1))


def _ada_kernel(c_ref, w_ref, b_ref, o_ref):
    cv = c_ref[...]
    act = cv * _sigmoid(cv)
    o_ref[...] = jnp.dot(act, w_ref[...], preferred_element_type=F32,
                         precision=lax.Precision.HIGHEST) + b_ref[...]


def _ada(cvec, ada_w, ada_b):
    r, d = cvec.shape
    n = ada_w.shape[1]
    tn = n // 4
    return pl.pallas_call(
        _ada_kernel,
        grid=(n // tn,),
        in_specs=[pl.BlockSpec((r, d), lambda i: (0, 0)),
                  pl.BlockSpec((d, tn), lambda i: (0, i)),
                  pl.BlockSpec((1, tn), lambda i: (0, i))],
        out_specs=pl.BlockSpec((r, tn), lambda i: (0, i)),
        out_shape=jax.ShapeDtypeStruct((r, n), F32),
        compiler_params=_cparams(("arbitrary",)),
        name="ada_mod",
    )(cvec, ada_w, ada_b.reshape(1, n))


def _shift_conv(scr, pieces, tm, w0, w1, w2):
    prev, cur, nxt = pieces
    scr[0:SUBLANES, :] = prev
    scr[SUBLANES:SUBLANES + tm, :] = cur
    scr[SUBLANES + tm:, :] = nxt
    return (w0 * scr[SUBLANES - 1:SUBLANES - 1 + tm, :] + w1 * cur
            + w2 * scr[SUBLANES + 1:SUBLANES + 1 + tm, :])


def _conv_mixer_kernel(h_ref, hp_ref, hn_ref, mod_ref, g_ref, win_ref, wc_ref, wout_ref,
                       o_ref, scr, *, n_tiles):
    t = pl.program_id(1)
    tm, d = h_ref.shape[1], h_ref.shape[2]
    hcur = h_ref[0]
    shift, scale, gate = mod_ref[0, 0:1, :], mod_ref[0, 1:2, :], mod_ref[0, 2:3, :]
    x_ext = jnp.concatenate([hcur, hp_ref[0], hn_ref[0]], axis=0)
    xn = _norm_mod(x_ext, g_ref[...], shift, scale).astype(BF16)
    b_gate = jnp.dot(xn[:tm], win_ref[:, 0:d], preferred_element_type=F32)
    c_gate = jnp.dot(xn, win_ref[:, d:2 * d], preferred_element_type=F32)
    hh = jnp.dot(xn, win_ref[:, 2 * d:3 * d], preferred_element_type=F32)
    ch = c_gate * hh
    prev = jnp.where(t > 0, ch[tm:tm + SUBLANES], 0.0)
    nxt = jnp.where(t < n_tiles - 1, ch[tm + SUBLANES:], 0.0)
    conv = _shift_conv(scr, (prev, ch[:tm], nxt), tm, wc_ref[0:1, :], wc_ref[1:2, :], wc_ref[2:3, :])
    z = (b_gate * conv).astype(BF16)
    y = jnp.dot(z, wout_ref[...], preferred_element_type=F32)
    o_ref[0] = hcur + gate * y


def _conv_mixer(h, mods, g1, w_in, w_conv, w_out):
    b, s, d = h.shape
    tm = _row_tile(s)
    n_tiles = s // tm
    cur, prev, nxt = _halo_specs(tm, d, s)
    return pl.pallas_call(
        functools.partial(_conv_mixer_kernel, n_tiles=n_tiles),
        grid=(b, n_tiles),
        in_specs=[cur, prev, nxt, _mod_spec(mods, d), _const_spec((1, d)),
                  _const_spec((d, 3 * d)), _const_spec((CONV_W, d)), _const_spec((d, d))],
        out_specs=pl.BlockSpec((1, tm, d), lambda bi, t: (bi, t, 0)),
        out_shape=jax.ShapeDtypeStruct((b, s, d), F32),
        scratch_shapes=[pltpu.VMEM((tm + 2 * SUBLANES, d), F32)],
        compiler_params=_cparams(("parallel", "parallel")),
        name="conv_mixer",
    )(h, h, h, mods, g1, w_in, w_conv, w_out)


def _ffn_chunk(dff):
    best = LANES
    for fc in range(LANES, 1408 + 1, LANES):
        if dff % fc == 0:
            best = fc
    return best


def _ffn_kernel(*refs, n_tiles, fc, fused_attn):
    if fused_attn:
        (h_ref, hp_ref, hn_ref, a_ref, ap_ref, an_ref, wo_ref,
         mod_ref, g_ref, wup_ref, cw_ref, cb_ref, wdn_ref, o_ref, scr) = refs
    else:
        h_ref, hp_ref, hn_ref, mod_ref, g_ref, wup_ref, cw_ref, cb_ref, wdn_ref, o_ref, scr = refs
    t = pl.program_id(1)
    tm, d = h_ref.shape[1], h_ref.shape[2]
    dff = wdn_ref.shape[0]
    hcur, hprev, hnext = h_ref[0], hp_ref[0], hn_ref[0]
    if fused_attn:
        a_ext = jnp.concatenate([a_ref[0], ap_ref[0], an_ref[0]], axis=0)
        y = jnp.dot(a_ext, wo_ref[...], preferred_element_type=F32)
        gate1 = mod_ref[0, 2:3, :]
        hcur = hcur + gate1 * y[:tm]
        hprev = hprev + gate1 * y[tm + BF16_ROWS - SUBLANES:tm + BF16_ROWS]
        hnext = hnext + gate1 * y[tm + BF16_ROWS:tm + BF16_ROWS + SUBLANES]
    shift, scale, gate = mod_ref[0, 3:4, :], mod_ref[0, 4:5, :], mod_ref[0, 5:6, :]
    x_ext = jnp.concatenate([hcur, hprev, hnext], axis=0)
    xn = _norm_mod(x_ext, g_ref[...], shift, scale).astype(BF16)
    acc = jnp.zeros((tm, d), F32)
    for c0 in range(0, dff, fc):
        a = jnp.dot(xn[:tm], wup_ref[:, c0:c0 + fc], preferred_element_type=F32)
        gg = jnp.dot(xn, wup_ref[:, dff + c0:dff + c0 + fc], preferred_element_type=F32)
        prev = jnp.where(t > 0, gg[tm:tm + SUBLANES], 0.0)
        nxt = jnp.where(t < n_tiles - 1, gg[tm + SUBLANES:], 0.0)
        gc = _shift_conv(scr, (prev, gg[:tm], nxt), tm, cw_ref[0:1, c0:c0 + fc],
                         cw_ref[1:2, c0:c0 + fc], cw_ref[2:3, c0:c0 + fc]) + cb_ref[:, c0:c0 + fc]
        u = (gc * _sigmoid(gc) * a).astype(BF16)
        acc = acc + jnp.dot(u, wdn_ref[c0:c0 + fc, :], preferred_element_type=F32)
    o_ref[0] = hcur + gate * acc


def _ffn(h, mods, g2, w_up, conv_w, conv_b, w_down, attn=None):
    b, s, d = h.shape
    dff = w_down.shape[0]
    tm = _row_tile(s)
    n_tiles = s // tm
    fc = _ffn_chunk(dff)
    cur, prev, nxt = _halo_specs(tm, d, s)
    in_specs = [cur, prev, nxt]
    args = [h, h, h]
    if attn is not None:
        o, w_out = attn
        k = o.shape[2]
        per = tm // BF16_ROWS
        last = s // BF16_ROWS - 1
        in_specs += [pl.BlockSpec((1, tm, k), lambda bi, t: (bi, t, 0)),
                     pl.BlockSpec((1, BF16_ROWS, k), lambda bi, t: (bi, jnp.maximum(t * per - 1, 0), 0)),
                     pl.BlockSpec((1, BF16_ROWS, k), lambda bi, t: (bi, jnp.minimum((t + 1) * per, last), 0)),
                     _const_spec((k, d))]
        args += [o, o, o, w_out]
    return pl.pallas_call(
        functools.partial(_ffn_kernel, n_tiles=n_tiles, fc=fc, fused_attn=attn is not None),
        grid=(b, n_tiles),
        in_specs=in_specs + [_mod_spec(mods, d), _const_spec((1, d)),
                             _const_spec((d, 2 * dff)), _const_spec((CONV_W, dff)), _const_spec((1, dff)),
                             _const_spec((dff, d))],
        out_specs=pl.BlockSpec((1, tm, d), lambda bi, t: (bi, t, 0)),
        out_shape=jax.ShapeDtypeStruct((b, s, d), F32),
        scratch_shapes=[pltpu.VMEM((tm + 2 * SUBLANES, fc), F32)],
        compiler_params=_cparams(("parallel", "parallel")),
        name="conv_ffn" if attn is None else "outproj_conv_ffn",
    )(*args, mods, g2, w_up, conv_w, conv_b, w_down)


def _head_sum_matrix():
    i = jnp.arange(MXU_DIM) // HEAD_DIM
    return (i[:, None] == i[None, :]).astype(BF16)


def _rope_tables(n_tokens):
    rows = n_tokens // GRID_W
    row = jnp.repeat(jnp.arange(rows, dtype=F32), GRID_W)
    col = jnp.tile(jnp.arange(GRID_W, dtype=F32), rows)
    m = HEAD_DIM // 4
    inv_freq = ROPE_BASE ** (-jnp.arange(m, dtype=F32) / m)
    ang = jnp.stack([row, col], axis=-1)[:, :, None] * inv_freq
    cos, sin = jnp.cos(ang), jnp.sin(ang)
    c64 = jnp.concatenate([cos[:, 0], cos[:, 0], cos[:, 1], cos[:, 1]], axis=-1)
    s64 = jnp.concatenate([-sin[:, 0], sin[:, 0], -sin[:, 1], sin[:, 1]], axis=-1)
    reps = LANES // HEAD_DIM
    return jnp.tile(c64, (1, reps)), jnp.tile(s64, (1, reps))


def _proj_kernel(*refs, n_norm, widths, transposed, rope):
    h_ref, mod_ref, g_ref, w_ref, bm_ref, gain_ref = refs[:6]
    pos = 6
    if rope:
        cos_ref, sin_ref = refs[6:8]
        pos = 8
    out_refs = refs[pos:]
    shift, scale = mod_ref[0, 0:1, :], mod_ref[0, 1:2, :]
    xn = _norm_mod(h_ref[0], g_ref[...], shift, scale).astype(BF16)
    u = jnp.dot(xn, w_ref[...], preferred_element_type=F32)

    starts = [sum(widths[:i]) for i in range(len(widths))]

    def store(c0, val):
        for ref, st, wd, tr in zip(out_refs, starts, widths, transposed):
            if st <= c0 < st + wd and tr:
                ref[0, c0 - st:c0 - st + val.shape[1], :] = val.T.astype(BF16)
            elif st <= c0 < st + wd:
                ref[0, :, c0 - st:c0 - st + val.shape[1]] = val.astype(BF16)

    if rope:
        cos_t, sin_t = cos_ref[...], sin_ref[...]
        lane = lax.broadcasted_iota(jnp.int32, (1, LANES), 1)
        upper = (lane & (HEAD_DIM // 4)) != 0
    for c0 in range(0, n_norm, MXU_DIM):
        xb = u[:, c0:c0 + MXU_DIM]
        ss = jnp.dot((xb * xb).astype(BF16), bm_ref[...], preferred_element_type=F32)
        xb = xb * lax.rsqrt(ss * (1.0 / HEAD_DIM) + EPS) * gain_ref[:, c0:c0 + MXU_DIM]
        for c1 in range(0, MXU_DIM, LANES):
            xs = xb[:, c1:c1 + LANES]
            if rope:
                partner = jnp.where(upper, pltpu.roll(xs, HEAD_DIM // 4, 1),
                                    pltpu.roll(xs, LANES - HEAD_DIM // 4, 1))
                xs = xs * cos_t + partner * sin_t
            store(c0 + c1, xs)
    for c0 in range(n_norm, sum(widths), LANES):
        store(c0, u[:, c0:c0 + LANES])


def _proj(h, mods, g1, w, gains, widths, transposed, n_norm, rope_tabs):
    b, s, d = h.shape
    n = w.shape[1]
    tm = _row_tile(s)
    rope = rope_tabs is not None
    in_specs = [pl.BlockSpec((1, tm, d), lambda bi, t: (bi, t, 0)), _mod_spec(mods, d),
                _const_spec((1, d)), _const_spec((d, n)), _const_spec((MXU_DIM, MXU_DIM)),
                _const_spec((1, n_norm))]
    args = [h, mods, g1, w, _head_sum_matrix(), gains]
    if rope:
        in_specs += [pl.BlockSpec((tm, LANES), lambda bi, t: (t, 0))] * 2
        args += list(rope_tabs)
    out_specs = [pl.BlockSpec((1, wd, tm), lambda bi, t: (bi, 0, t)) if tr
                 else pl.BlockSpec((1, tm, wd), lambda bi, t: (bi, t, 0)) for wd, tr in zip(widths, transposed)]
    out_shape = [jax.ShapeDtypeStruct((b, wd, s) if tr else (b, s, wd), BF16)
                 for wd, tr in zip(widths, transposed)]
    return pl.pallas_call(
        functools.partial(_proj_kernel, n_norm=n_norm, widths=tuple(widths),
                          transposed=tuple(transposed), rope=rope),
        grid=(b, s // tm),
        in_specs=in_specs,
        out_specs=out_specs,
        out_shape=out_shape,
        compiler_params=_cparams(("parallel", "parallel")),
        name="qkv_proj_rope" if rope else "qkv_proj",
    )(*args)


def _lane_block_max(x, acc=None):
    for c0 in range(0, x.shape[1], LANES):
        blk = x[:, c0:c0 + LANES]
        acc = blk if acc is None else jnp.maximum(acc, blk)
    return acc


def _wa_attn_kernel(*refs, n_blocks, band):
    if band:
        sink_ref, q_ref, kp_ref, kc_ref, kn_ref, vp_ref, vc_ref, vn_ref, kx_ref, vx_ref, o_ref = refs
    else:
        sink_ref, q_ref, kx_ref, vx_ref, o_ref = refs
    j = pl.program_id(1)
    tq = q_ref.shape[1]
    tb = WINDOW if band else tq
    n_kv = vx_ref.shape[2] // LANES
    rows = WA_GROUP * tb
    lane = lax.broadcasted_iota(jnp.int32, (1, LANES), 1)
    lo = lane < HEAD_DIM
    lo_bf = jnp.where(lo, 1.0, 0.0).astype(BF16)
    hi_bf = jnp.where(lo, 0.0, 1.0).astype(BF16)
    if band:
        n_band = tb + 2 * WINDOW
        r = lax.broadcasted_iota(jnp.int32, (rows, n_band), 0) & (tb - 1)
        c = lax.broadcasted_iota(jnp.int32, (rows, n_band), 1)
        in_window = (c - r).astype(jnp.uint32) <= jnp.uint32(2 * WINDOW)
        k_all = [jnp.concatenate([kp_ref[0, kv * LANES:(kv + 1) * LANES, :],
                                  kc_ref[0, kv * LANES:(kv + 1) * LANES, :],
                                  kn_ref[0, kv * LANES:(kv + 1) * LANES, :]], axis=1) for kv in range(n_kv)]
        v_all = [jnp.concatenate([vp_ref[0, :, kv * LANES:(kv + 1) * LANES],
                                  vc_ref[0, :, kv * LANES:(kv + 1) * LANES],
                                  vn_ref[0, :, kv * LANES:(kv + 1) * LANES]], axis=0) for kv in range(n_kv)]
    for sub in range(tq // tb):
        if band:
            ok = in_window
            if sub == 0:
                ok = ok & (c >= jnp.where(j > 0, 0, WINDOW))
            if sub == tq // tb - 1:
                ok = ok & (c < jnp.where(j < n_blocks - 1, n_band, WINDOW + tb))
        q = q_ref[0, sub * tb:(sub + 1) * tb, :]
        for pair in range(n_kv // 2):
            probs, vals, stats = [], [], []
            for kv in (2 * pair, 2 * pair + 1):
                cs = slice(kv * LANES, (kv + 1) * LANES)
                qb0 = q[:, (2 * kv) * LANES:(2 * kv + 1) * LANES]
                qb1 = q[:, (2 * kv + 1) * LANES:(2 * kv + 2) * LANES]
                q4 = jnp.concatenate([qb0 * lo_bf, qb0 * hi_bf, qb1 * lo_bf, qb1 * hi_bf], axis=0)
                sk = jnp.concatenate([jnp.full((tb, 1), sink_ref[WA_GROUP * kv + g] * LOG2E, F32)
                                      for g in range(WA_GROUP)], axis=0)
                s_x = jnp.dot(q4, kx_ref[0, cs, :], preferred_element_type=F32)
                blk_max = _lane_block_max(s_x)
                v_rows = [vx_ref[0, :, cs]]
                if band:
                    s_b = jnp.dot(q4, k_all[kv][:, sub * tb:sub * tb + n_band], preferred_element_type=F32)
                    s_b = jnp.where(ok, s_b, NEG)
                    blk_max = _lane_block_max(s_b, blk_max)
                    v_rows = [v_all[kv][sub * tb:sub * tb + n_band]] + v_rows
                m = jnp.maximum(jnp.max(blk_max, axis=1, keepdims=True), sk)
                p = [jnp.exp2((s_x - m).astype(BF16))]
                if band:
                    p = [jnp.exp2((s_b - m).astype(BF16))] + p
                v_cat = jnp.concatenate(v_rows, axis=0)
                probs.append(jnp.concatenate(p, axis=1))
                vals.append(jnp.where(lo, v_cat, jnp.ones_like(v_cat)))
                stats.append((m, sk))
            zeros = jnp.zeros_like(vals[0])
            v_diag = jnp.concatenate([jnp.concatenate([vals[0], zeros], axis=1),
                                      jnp.concatenate([zeros, vals[1]], axis=1)], axis=0)
            o2 = jnp.dot(jnp.concatenate(probs, axis=1), v_diag, preferred_element_type=F32)
            for i, kv in enumerate((2 * pair, 2 * pair + 1)):
                oa = o2[:, i * LANES:(i + 1) * LANES]
                m, sk = stats[i]
                on = oa / (oa[:, HEAD_DIM:HEAD_DIM + 1] + jnp.exp2(sk - m))
                for half in range(2):
                    even = on[(2 * half) * tb:(2 * half + 1) * tb]
                    odd = on[(2 * half + 1) * tb:(2 * half + 2) * tb]
                    blk = jnp.where(lo, even, pltpu.roll(odd, HEAD_DIM, 1))
                    o_ref[0, sub * tb:(sub + 1) * tb,
                          (2 * kv + half) * LANES:(2 * kv + half + 1) * LANES] = blk.astype(BF16)


WA_TQ = 512


def _wa_attn(sink, q, kt, v, kxt, vx, band):
    b, s, d = q.shape
    nk = vx.shape[2]
    n_ctx = vx.shape[1]
    smem = pl.BlockSpec(memory_space=pltpu.SMEM)
    if band:
        tq = WA_TQ if s % WA_TQ == 0 else WINDOW
        nb = s // tq
        per = tq // WINDOW
        last = s // WINDOW - 1
        before = lambda j: jnp.maximum(j * per - 1, 0)
        after = lambda j: jnp.minimum((j + 1) * per, last)
        k_specs = [pl.BlockSpec((1, nk, WINDOW), lambda bi, j: (bi, 0, before(j))),
                   pl.BlockSpec((1, nk, tq), lambda bi, j: (bi, 0, j)),
                   pl.BlockSpec((1, nk, WINDOW), lambda bi, j: (bi, 0, after(j)))]
        v_specs = [pl.BlockSpec((1, WINDOW, nk), lambda bi, j: (bi, before(j), 0)),
                   pl.BlockSpec((1, tq, nk), lambda bi, j: (bi, j, 0)),
                   pl.BlockSpec((1, WINDOW, nk), lambda bi, j: (bi, after(j), 0))]
        in_specs = [smem, pl.BlockSpec((1, tq, d), lambda bi, j: (bi, j, 0))] + k_specs + v_specs
        args = [sink, q, kt, kt, kt, v, v, v]
    else:
        tq = _row_tile(s)
        nb = s // tq
        in_specs = [smem, pl.BlockSpec((1, tq, d), lambda bi, j: (bi, j, 0))]
        args = [sink, q]
    in_specs += [pl.BlockSpec((1, nk, n_ctx), lambda bi, j: (bi, 0, 0)),
                 pl.BlockSpec((1, n_ctx, nk), lambda bi, j: (bi, 0, 0))]
    args += [kxt, vx]
    return pl.pallas_call(
        functools.partial(_wa_attn_kernel, n_blocks=nb, band=band),
        grid=(b, nb),
        in_specs=in_specs,
        out_specs=pl.BlockSpec((1, tq, d), lambda bi, j: (bi, j, 0)),
        out_shape=jax.ShapeDtypeStruct((b, s, d), BF16),
        compiler_params=_cparams(("parallel", "parallel")),
        name="window_gqa" if band else "ctx_gqa",
    )(*args)


DA_TK = 512
DA_TQ = 1024


def _da_attn_kernel(lam_ref, subg_ref, q_ref, kt_ref, v_ref, kxt_ref, vx_ref, o_ref, s_scr, *, lam_init):
    tq = q_ref.shape[1]
    s_len = v_ref.shape[1]
    n_ctx = vx_ref.shape[1]
    lane = lax.broadcasted_iota(jnp.int32, (1, LANES), 1)
    lo = lane < HEAD_DIM
    lo_bf = jnp.where(lo, 1.0, 0.0).astype(BF16)
    hi_bf = jnp.where(lo, 0.0, 1.0).astype(BF16)
    q = q_ref[0]
    qq = jnp.concatenate([q * lo_bf, q * hi_bf], axis=0)

    chunks = [(kt_ref, v_ref, r0, min(DA_TK, s_len - r0)) for r0 in range(0, s_len, DA_TK)]
    chunks.append((kxt_ref, vx_ref, 0, n_ctx))

    blk_max = None
    base = 0
    for kref, _, r0, n in chunks:
        s = jnp.dot(qq, kref[0, :, r0:r0 + n], preferred_element_type=F32)
        s_scr[:, base:base + n] = s
        blk_max = _lane_block_max(s, blk_max)
        base += n
    m = jnp.max(blk_max, axis=1, keepdims=True)
    acc = [jnp.zeros((tq, 2 * LANES), F32) for _ in range(2)]
    base = 0
    for _, vref, r0, n in chunks:
        v_aug = jnp.concatenate([vref[0, r0:r0 + n, :], jnp.ones((n, LANES), BF16)], axis=1)
        for comp in range(2):
            rs = slice(comp * tq, (comp + 1) * tq)
            p = jnp.exp2((s_scr[rs, base:base + n] - m[rs]).astype(BF16))
            acc[comp] = acc[comp] + jnp.dot(p, v_aug, preferred_element_type=F32)
        base += n
    on = [a[:, :LANES] / a[:, LANES:] for a in acc]
    lam = (jnp.exp(jnp.sum(lam_ref[0:1, :] * lam_ref[1:2, :], axis=-1, keepdims=True))
           - jnp.exp(jnp.sum(lam_ref[2:3, :] * lam_ref[3:4, :], axis=-1, keepdims=True)) + lam_init)
    o = on[0] - lam * on[1]
    o = o * lax.rsqrt(jnp.mean(o * o, axis=-1, keepdims=True) + EPS) * subg_ref[...]
    o_ref[0] = (o * (1.0 - lam_init)).astype(BF16)


def _da_attn(lam_rows, sub_g, q, kt, v, kxt, vx, lam_init):
    b, s, d = q.shape
    n_ctx = vx.shape[1]
    n_heads = d // LANES
    tq = DA_TQ if s % DA_TQ == 0 else _row_tile(s)
    head_rows = lambda n: pl.BlockSpec((1, n, LANES), lambda bi, hi, qi: (bi, 0, hi))
    head_cols = lambda n: pl.BlockSpec((1, LANES, n), lambda bi, hi, qi: (bi, hi, 0))
    return pl.pallas_call(
        functools.partial(_da_attn_kernel, lam_init=lam_init),
        grid=(b, n_heads, s // tq),
        in_specs=[pl.BlockSpec((SUBLANES, LANES), lambda bi, hi, qi: (0, 0)),
                  pl.BlockSpec((1, LANES), lambda bi, hi, qi: (0, 0)),
                  pl.BlockSpec((1, tq, LANES), lambda bi, hi, qi: (bi, qi, hi)),
                  head_cols(s), head_rows(s), head_cols(n_ctx), head_rows(n_ctx)],
        out_specs=pl.BlockSpec((1, tq, LANES), lambda bi, hi, qi: (bi, qi, hi)),
        out_shape=jax.ShapeDtypeStruct((b, s, d), BF16),
        scratch_shapes=[pltpu.VMEM((2 * tq, s + n_ctx), F32)],
        compiler_params=_cparams(("parallel", "parallel", "arbitrary")),
        name="diff_attn",
    )(lam_rows, sub_g, q, kt, v, kxt, vx)


def _widen_heads(w, n_heads, dup):
    d = w.shape[0]
    w = w.reshape(d, n_heads, HEAD_DIM)
    other = w if dup else jnp.zeros_like(w)
    return jnp.concatenate([w, other], axis=-1).reshape(d, n_heads * LANES)


def _wa_layer(h, hc, mods, mods_c, g1, w_qkv, q_g, k_g, sink, w_out, rope_tabs):
    d = h.shape[2]
    n_q = d // HEAD_DIM
    n_kv = n_q // WA_GROUP
    wq = w_qkv[:, :d]
    wk = _widen_heads(w_qkv[:, d:d + n_kv * HEAD_DIM], n_kv, dup=True)
    wv = _widen_heads(w_qkv[:, d + n_kv * HEAD_DIM:], n_kv, dup=False)
    w = jnp.concatenate([wq, wk, wv], axis=1).astype(BF16)
    nk = n_kv * LANES
    gains = jnp.concatenate([jnp.tile(q_g, n_q) * (HEAD_DIM ** -0.5 * LOG2E),
                             jnp.tile(k_g, 2 * n_kv)])[None, :]
    widths = (d, nk, nk)
    tr = (False, True, False)
    q, k, v = _proj(h, mods, g1, w, gains, widths, tr, d + nk, rope_tabs)
    qc, kc, vc = _proj(hc, mods_c, g1, w, gains, widths, tr, d + nk, None)
    o = _wa_attn(sink, q, k, v, kc, vc, band=True)
    oc = _wa_attn(sink, qc, None, None, kc, vc, band=False)
    return o, oc, w_out.astype(BF16)


def _da_layer(h, hc, mods, mods_c, g1, w_qkv, q_g, k_g, lq1, lk1, lq2, lk2, sub_g, w_out,
              layer_idx, rope_tabs):
    d = h.shape[2]
    n_qk = 2 * d // HEAD_DIM
    w = w_qkv.astype(BF16)
    gains = jnp.concatenate([jnp.tile(q_g, n_qk // 2) * (HEAD_DIM ** -0.5 * LOG2E),
                             jnp.tile(k_g, n_qk // 2)])[None, :]
    widths = (d, d, d)
    tr = (False, True, False)
    q, k, v = _proj(h, mods, g1, w, gains, widths, tr, 2 * d, rope_tabs)
    _, kc, vc = _proj(hc, mods_c, g1, w, gains, widths, tr, 2 * d, None)
    lam_init = 0.8 - 0.6 * math.exp(-0.3 * layer_idx)
    lam_rows = jnp.zeros((SUBLANES, LANES), F32).at[:4, :HEAD_DIM].set(jnp.stack([lq1, lk1, lq2, lk2]))
    o = _da_attn(lam_rows, sub_g[None, :], q, k, v, kc, vc, lam_init)
    return o, w_out.astype(BF16)


def kernel(x, c, ctx, c_ctx, l0_ada_w, l0_ada_b, l0_norm1, l0_norm2, l0_sc_in, l0_sc_conv, l0_sc_out, l0_ffn_up, l0_ffn_conv_w, l0_ffn_conv_b, l0_ffn_down, l1_ada_w, l1_ada_b, l1_norm1, l1_norm2, l1_wa_qkv, l1_wa_qnorm, l1_wa_knorm, l1_wa_sink, l1_wa_out, l1_ffn_up, l1_ffn_conv_w, l1_ffn_conv_b, l1_ffn_down, l2_ada_w, l2_ada_b, l2_norm1, l2_norm2, l2_da_qkv, l2_da_qnorm, l2_da_knorm, l2_da_lq1, l2_da_lk1, l2_da_lq2, l2_da_lk2, l2_da_subln, l2_da_out, l2_ffn_up, l2_ffn_conv_w, l2_ffn_conv_b, l2_ffn_down, l3_ada_w, l3_ada_b, l3_norm1, l3_norm2, l3_sc_in, l3_sc_conv, l3_sc_out, l3_ffn_up, l3_ffn_conv_w, l3_ffn_conv_b, l3_ffn_down):
    commons = [(l0_ada_w, l0_ada_b, l0_norm1, l0_norm2),
               (l1_ada_w, l1_ada_b, l1_norm1, l1_norm2),
               (l2_ada_w, l2_ada_b, l2_norm1, l2_norm2),
               (l3_ada_w, l3_ada_b, l3_norm1, l3_norm2)]
    mixers = [(l0_sc_in, l0_sc_conv, l0_sc_out),
              (l1_wa_qkv, l1_wa_qnorm, l1_wa_knorm, l1_wa_sink, l1_wa_out),
              (l2_da_qkv, l2_da_qnorm, l2_da_knorm, l2_da_lq1, l2_da_lk1, l2_da_lq2, l2_da_lk2,
               l2_da_subln, l2_da_out),
              (l3_sc_in, l3_sc_conv, l3_sc_out)]
    ffns = [(l0_ffn_up, l0_ffn_conv_w, l0_ffn_conv_b, l0_ffn_down),
            (l1_ffn_up, l1_ffn_conv_w, l1_ffn_conv_b, l1_ffn_down),
            (l2_ffn_up, l2_ffn_conv_w, l2_ffn_conv_b, l2_ffn_down),
            (l3_ffn_up, l3_ffn_conv_w, l3_ffn_conv_b, l3_ffn_down)]
    depth = len(commons)
    bsz, seq, d = x.shape
    rope_tabs = _rope_tables(seq)

    n_rows = -(-(bsz + 1) // SUBLANES) * SUBLANES
    cvec = jnp.zeros((n_rows, d), F32).at[:bsz].set(c).at[bsz].set(c_ctx)

    h, hc = x, ctx
    for i in range(depth):
        kind = i % N_MIXERS
        ada_w, ada_b, g1, g2 = commons[i]
        ctx_after = any(j % N_MIXERS != 0 for j in range(i + 1, depth))
        mods_all = _ada(cvec, ada_w, ada_b).reshape(n_rows, 6, d)
        mods, mods_c = mods_all[:bsz], mods_all[bsz:bsz + 1]
        g1, g2 = g1[None, :], g2[None, :]

        attn = attn_c = None
        if kind == 0:
            w_in, w_conv, w_out = mixers[i]
            w_in, w_out = w_in.astype(BF16), w_out.astype(BF16)
            h = _conv_mixer(h, mods, g1, w_in, w_conv, w_out)
            if ctx_after:
                hc = _conv_mixer(hc, mods_c, g1, w_in, w_conv, w_out)
        elif kind == 1:
            o, oc, w_out = _wa_layer(h, hc, mods, mods_c, g1, *mixers[i], rope_tabs)
            attn, attn_c = (o, w_out), (oc, w_out)
        else:
            o, w_out = _da_layer(h, hc, mods, mods_c, g1, *mixers[i], i, rope_tabs)
            attn = (o, w_out)
            assert not ctx_after

        w_up, conv_w, conv_b, w_down = ffns[i]
        w_up, w_down = w_up.astype(BF16), w_down.astype(BF16)
        conv_b = conv_b[None, :]
        h = _ffn(h, mods, g2, w_up, conv_w, conv_b, w_down, attn)
        if ctx_after:
            hc = _ffn(hc, mods_c, g2, w_up, conv_w, conv_b, w_down, attn_c)
    return h
```

```python
import functools
import math

import jax
import jax.numpy as jnp
from jax import lax
from jax.experimental import pallas as pl
from jax.experimental.pallas import tpu as pltpu

F32 = jnp.float32
BF16 = jnp.bfloat16

LANES = 128
SUBLANES = 8
BF16_ROWS = 16
MXU_DIM = 256
VMEM_LIMIT_BYTES = 56 * 1024 * 1024

GRID_W = 64
HEAD_DIM = 64
WA_GROUP = 4
WINDOW = 128
CONV_W = 3
N_MIXERS = 3
ROPE_BASE = 10000.0
EPS = 1e-6
NEG = -1e30

NT_DIMS = (((1,), (1,)), ((), ()))
LOG2E = 1.4426950408889634


def _cparams(sem):
    return pltpu.CompilerParams(dimension_semantics=sem, vmem_limit_bytes=VMEM_LIMIT_BYTES)


def _sigmoid(x):
    return 1.0 / (1.0 + jnp.exp(-x))


def _norm_mod(x, g, shift, scale):
    ms = jnp.mean(x * x, axis=-1, keepdims=True)
    y = x * lax.rsqrt(ms + EPS) * g
    return y * (1.0 + scale) + shift


def _row_tile(s):
    for tm in (512, 256, 128):
        if s % tm == 0:
            return tm
    raise ValueError(f"sequence length {s} must be a multiple of 128")


def _halo_specs(tm, d, s):
    per = tm // SUBLANES
    last = s // SUBLANES - 1
    cur = pl.BlockSpec((1, tm, d), lambda b, t: (b, t, 0))
    prev = pl.BlockSpec((1, SUBLANES, d), lambda b, t: (b, jnp.maximum(t * per - 1, 0), 0))
    nxt = pl.BlockSpec((1, SUBLANES, d), lambda b, t: (b, jnp.minimum((t + 1) * per, last), 0))
    return cur, prev, nxt


def _mod_spec(mods, d):
    if mods.shape[0] == 1:
        return pl.BlockSpec((1, 6, d), lambda b, t: (0, 0, 0))
    return pl.BlockSpec((1, 6, d), lambda b, t: (b, 0, 0))


def _const_spec(shape):
    zeros = (0,) * len(shape)
    return pl.BlockSpec(shape, lambda *_: zeros, pipeline_mode=pl.Buffered(1))


def _ada_kernel(c_ref, w_ref, b_ref, o_ref):
    cv = c_ref[...]
    act = cv * _sigmoid(cv)
    o_ref[...] = jnp.dot(act, w_ref[...], preferred_element_type=F32,
                         precision=lax.Precision.HIGHEST) + b_ref[...]


def _ada(cvec, ada_w, ada_b):
    r, d = cvec.shape
    n = ada_w.shape[1]
    tn = n // 4
    return pl.pallas_call(
        _ada_kernel,
        grid=(n // tn,),
        in_specs=[pl.BlockSpec((r, d), lambda i: (0, 0)),
                  pl.BlockSpec((d, tn), lambda i: (0, i)),
                  pl.BlockSpec((1, tn), lambda i: (0, i))],
        out_specs=pl.BlockSpec((r, tn), lambda i: (0, i)),
        out_shape=jax.ShapeDtypeStruct((r, n), F32),
        compiler_params=_cparams(("arbitrary",)),
        name="ada_mod",
    )(cvec, ada_w, ada_b.reshape(1, n))


def _shift_conv(scr, pieces, tm, w0, w1, w2):
    prev, cur, nxt = pieces
    scr[0:SUBLANES, :] = prev
    scr[SUBLANES:SUBLANES + tm, :] = cur
    scr[SUBLANES + tm:, :] = nxt
    return (w0 * scr[SUBLANES - 1:SUBLANES - 1 + tm, :] + w1 * cur
            + w2 * scr[SUBLANES + 1:SUBLANES + 1 + tm, :])


def _conv_mixer_kernel(h_ref, hp_ref, hn_ref, mod_ref, g_ref, win_ref, wc_ref, wout_ref,
                       o_ref, scr, *, n_tiles):
    t = pl.program_id(1)
    tm, d = h_ref.shape[1], h_ref.shape[2]
    hcur = h_ref[0]
    shift, scale, gate = mod_ref[0, 0:1, :], mod_ref[0, 1:2, :], mod_ref[0, 2:3, :]
    x_ext = jnp.concatenate([hcur, hp_ref[0], hn_ref[0]], axis=0)
    xn = _norm_mod(x_ext, g_ref[...], shift, scale).astype(BF16)
    b_gate = jnp.dot(xn[:tm], win_ref[:, 0:d], preferred_element_type=F32)
    c_gate = jnp.dot(xn, win_ref[:, d:2 * d], preferred_element_type=F32)
    hh = jnp.dot(xn, win_ref[:, 2 * d:3 * d], preferred_element_type=F32)
    ch = c_gate * hh
    prev = jnp.where(t > 0, ch[tm:tm + SUBLANES], 0.0)
    nxt = jnp.where(t < n_tiles - 1, ch[tm + SUBLANES:], 0.0)
    conv = _shift_conv(scr, (prev, ch[:tm], nxt), tm, wc_ref[0:1, :], wc_ref[1:2, :], wc_ref[2:3, :])
    z = (b_gate * conv).astype(BF16)
    y = jnp.dot(z, wout_ref[...], preferred_element_type=F32)
    o_ref[0] = hcur + gate * y


def _conv_mixer(h, mods, g1, w_in, w_conv, w_out):
    b, s, d = h.shape
    tm = _row_tile(s)
    n_tiles = s // tm
    cur, prev, nxt = _halo_specs(tm, d, s)
    return pl.pallas_call(
        functools.partial(_conv_mixer_kernel, n_tiles=n_tiles),
        grid=(b, n_tiles),
        in_specs=[cur, prev, nxt, _mod_spec(mods, d), _const_spec((1, d)),
                  _const_spec((d, 3 * d)), _const_spec((CONV_W, d)), _const_spec((d, d))],
        out_specs=pl.BlockSpec((1, tm, d), lambda bi, t: (bi, t, 0)),
        out_shape=jax.ShapeDtypeStruct((b, s, d), F32),
        scratch_shapes=[pltpu.VMEM((tm + 2 * SUBLANES, d), F32)],
        compiler_params=_cparams(("parallel", "parallel")),
        name="conv_mixer",
    )(h, h, h, mods, g1, w_in, w_conv, w_out)


FFN_CHUNK = 2816


def _ffn_chunks(dff):
    return [(c0, min(FFN_CHUNK, dff - c0)) for c0 in range(0, dff, FFN_CHUNK)]


def _ffn_kernel(*refs, n_tiles, chunks, fused_attn):
    if fused_attn:
        (h_ref, hp_ref, hn_ref, a_ref, ap_ref, an_ref, wo_ref,
         mod_ref, g_ref, wup_ref, cw_ref, cb_ref, wdn_ref, o_ref, scr) = refs
    else:
        h_ref, hp_ref, hn_ref, mod_ref, g_ref, wup_ref, cw_ref, cb_ref, wdn_ref, o_ref, scr = refs
    t = pl.program_id(1)
    tm, d = h_ref.shape[1], h_ref.shape[2]
    dff = wdn_ref.shape[0]
    hcur, hprev, hnext = h_ref[0], hp_ref[0], hn_ref[0]
    if fused_attn:
        a_ext = jnp.concatenate([a_ref[0], ap_ref[0], an_ref[0]], axis=0)
        y = jnp.dot(a_ext, wo_ref[...], preferred_element_type=F32)
        gate1 = mod_ref[0, 2:3, :]
        hcur = hcur + gate1 * y[:tm]
        hprev = hprev + gate1 * y[tm + BF16_ROWS - SUBLANES:tm + BF16_ROWS]
        hnext = hnext + gate1 * y[tm + BF16_ROWS:tm + BF16_ROWS + SUBLANES]
    shift, scale, gate = mod_ref[0, 3:4, :], mod_ref[0, 4:5, :], mod_ref[0, 5:6, :]
    x_ext = jnp.concatenate([hcur, hprev, hnext], axis=0)
    xn = _norm_mod(x_ext, g_ref[...], shift, scale).astype(BF16)
    acc = jnp.zeros((tm, d), F32)
    for c0, fc in chunks:
        a = jnp.dot(xn[:tm], wup_ref[:, c0:c0 + fc], preferred_element_type=F32)
        gg = jnp.dot(xn, wup_ref[:, dff + c0:dff + c0 + fc], preferred_element_type=F32)
        prev = jnp.where(t > 0, gg[tm:tm + SUBLANES], 0.0)
        nxt = jnp.where(t < n_tiles - 1, gg[tm + SUBLANES:], 0.0)
        gc = _shift_conv(scr.at[:, 0:fc], (prev, gg[:tm], nxt), tm, cw_ref[0:1, c0:c0 + fc],
                         cw_ref[1:2, c0:c0 + fc], cw_ref[2:3, c0:c0 + fc]) + cb_ref[:, c0:c0 + fc]
        u = (gc * _sigmoid(gc) * a).astype(BF16)
        acc = acc + jnp.dot(u, wdn_ref[c0:c0 + fc, :], preferred_element_type=F32)
    o_ref[0] = hcur + gate * acc


def _ffn(h, mods, g2, w_up, conv_w, conv_b, w_down, attn=None):
    b, s, d = h.shape
    dff = w_down.shape[0]
    tm = _row_tile(s)
    n_tiles = s // tm
    chunks = _ffn_chunks(dff)
    cur, prev, nxt = _halo_specs(tm, d, s)
    in_specs = [cur, prev, nxt]
    args = [h, h, h]
    if attn is not None:
        o, w_out = attn
        k = o.shape[2]
        per = tm // BF16_ROWS
        last = s // BF16_ROWS - 1
        in_specs += [pl.BlockSpec((1, tm, k), lambda bi, t: (bi, t, 0)),
                     pl.BlockSpec((1, BF16_ROWS, k), lambda bi, t: (bi, jnp.maximum(t * per - 1, 0), 0)),
                     pl.BlockSpec((1, BF16_ROWS, k), lambda bi, t: (bi, jnp.minimum((t + 1) * per, last), 0)),
                     _const_spec((k, d))]
        args += [o, o, o, w_out]
    return pl.pallas_call(
        functools.partial(_ffn_kernel, n_tiles=n_tiles, chunks=tuple(chunks), fused_attn=attn is not None),
        grid=(b, n_tiles),
        in_specs=in_specs + [_mod_spec(mods, d), _const_spec((1, d)),
                             _const_spec((d, 2 * dff)), _const_spec((CONV_W, dff)), _const_spec((1, dff)),
                             _const_spec((dff, d))],
        out_specs=pl.BlockSpec((1, tm, d), lambda bi, t: (bi, t, 0)),
        out_shape=jax.ShapeDtypeStruct((b, s, d), F32),
        scratch_shapes=[pltpu.VMEM((tm + 2 * SUBLANES, max(fc for _, fc in chunks)), F32)],
        compiler_params=_cparams(("parallel", "parallel")),
        name="conv_ffn" if attn is None else "outproj_conv_ffn",
    )(*args, mods, g2, w_up, conv_w, conv_b, w_down)


def _head_sum_matrix():
    i = jnp.arange(MXU_DIM) // HEAD_DIM
    return (i[:, None] == i[None, :]).astype(BF16)


def _rope_tables(n_tokens):
    rows = n_tokens // GRID_W
    row = jnp.repeat(jnp.arange(rows, dtype=F32), GRID_W)
    col = jnp.tile(jnp.arange(GRID_W, dtype=F32), rows)
    m = HEAD_DIM // 4
    inv_freq = ROPE_BASE ** (-jnp.arange(m, dtype=F32) / m)
    ang = jnp.stack([row, col], axis=-1)[:, :, None] * inv_freq
    cos, sin = jnp.cos(ang), jnp.sin(ang)
    c64 = jnp.concatenate([cos[:, 0], cos[:, 0], cos[:, 1], cos[:, 1]], axis=-1)
    s64 = jnp.concatenate([-sin[:, 0], sin[:, 0], -sin[:, 1], sin[:, 1]], axis=-1)
    reps = LANES // HEAD_DIM
    return jnp.tile(c64, (1, reps)), jnp.tile(s64, (1, reps))


PROJ_SUBTILES = 4


def _proj_kernel(*refs, n_norm, widths, transposed, rope):
    h_ref, mod_ref, g_ref, w_ref, bm_ref, gain_ref = refs[:6]
    pos = 6
    if rope:
        cos_ref, sin_ref = refs[6:8]
        pos = 8
    out_refs = refs[pos:]
    shift, scale = mod_ref[0, 0:1, :], mod_ref[0, 1:2, :]
    starts = [sum(widths[:i]) for i in range(len(widths))]
    if rope:
        lane = lax.broadcasted_iota(jnp.int32, (1, LANES), 1)
        upper = (lane & (HEAD_DIM // 4)) != 0
    tm = h_ref.shape[1]
    sub = tm // PROJ_SUBTILES
    for r0 in range(0, tm, sub):
        rows = slice(r0, r0 + sub)
        xn = _norm_mod(h_ref[0, rows, :], g_ref[...], shift, scale).astype(BF16)
        u = jnp.dot(xn, w_ref[...], preferred_element_type=F32)

        def store(c0, val):
            for ref, st, wd, tr in zip(out_refs, starts, widths, transposed):
                if st <= c0 < st + wd and tr:
                    ref[0, c0 - st:c0 - st + val.shape[1], rows] = val.T.astype(BF16)
                elif st <= c0 < st + wd:
                    ref[0, rows, c0 - st:c0 - st + val.shape[1]] = val.astype(BF16)

        if rope:
            cos_t, sin_t = cos_ref[rows, :], sin_ref[rows, :]
        for c0 in range(0, n_norm, MXU_DIM):
            xb = u[:, c0:c0 + MXU_DIM]
            ss = jnp.dot((xb * xb).astype(BF16), bm_ref[...], preferred_element_type=F32)
            xb = xb * lax.rsqrt(ss * (1.0 / HEAD_DIM) + EPS) * gain_ref[:, c0:c0 + MXU_DIM]
            for c1 in range(0, MXU_DIM, LANES):
                xs = xb[:, c1:c1 + LANES]
                if rope:
                    partner = jnp.where(upper, pltpu.roll(xs, HEAD_DIM // 4, 1),
                                        pltpu.roll(xs, LANES - HEAD_DIM // 4, 1))
                    xs = xs * cos_t + partner * sin_t
                store(c0 + c1, xs)
        for c0 in range(n_norm, sum(widths), LANES):
            store(c0, u[:, c0:c0 + LANES])


def _proj(h, mods, g1, w, gains, widths, transposed, n_norm, rope_tabs):
    b, s, d = h.shape
    n = w.shape[1]
    tm = _row_tile(s)
    rope = rope_tabs is not None
    in_specs = [pl.BlockSpec((1, tm, d), lambda bi, t: (bi, t, 0)), _mod_spec(mods, d),
                _const_spec((1, d)), _const_spec((d, n)), _const_spec((MXU_DIM, MXU_DIM)),
                _const_spec((1, n_norm))]
    args = [h, mods, g1, w, _head_sum_matrix(), gains]
    if rope:
        in_specs += [pl.BlockSpec((tm, LANES), lambda bi, t: (t, 0))] * 2
        args += list(rope_tabs)
    out_specs = [pl.BlockSpec((1, wd, tm), lambda bi, t: (bi, 0, t)) if tr
                 else pl.BlockSpec((1, tm, wd), lambda bi, t: (bi, t, 0)) for wd, tr in zip(widths, transposed)]
    out_shape = [jax.ShapeDtypeStruct((b, wd, s) if tr else (b, s, wd), BF16)
                 for wd, tr in zip(widths, transposed)]
    return pl.pallas_call(
        functools.partial(_proj_kernel, n_norm=n_norm, widths=tuple(widths),
                          transposed=tuple(transposed), rope=rope),
        grid=(b, s // tm),
        in_specs=in_specs,
        out_specs=out_specs,
        out_shape=out_shape,
        compiler_params=_cparams(("parallel", "parallel")),
        name="qkv_proj_rope" if rope else "qkv_proj",
    )(*args)


def _lane_block_max(x, acc=None):
    for c0 in range(0, x.shape[1], LANES):
        blk = x[:, c0:c0 + LANES]
        acc = blk if acc is None else jnp.maximum(acc, blk)
    return acc


def _wa_attn_kernel(*refs, n_blocks, band):
    if band:
        sink_ref, q_ref, kp_ref, kc_ref, kn_ref, vp_ref, vc_ref, vn_ref, kx_ref, vx_ref, o_ref = refs
    else:
        sink_ref, q_ref, kx_ref, vx_ref, o_ref = refs
    j = pl.program_id(1)
    tq = q_ref.shape[1]
    tb = WINDOW if band else tq
    n_kv = vx_ref.shape[2] // LANES
    rows = WA_GROUP * tb
    lane = lax.broadcasted_iota(jnp.int32, (1, LANES), 1)
    lo = lane < HEAD_DIM
    lo_bf = jnp.where(lo, 1.0, 0.0).astype(BF16)
    hi_bf = jnp.where(lo, 0.0, 1.0).astype(BF16)
    if band:
        n_band = tb + 2 * WINDOW
        r = lax.broadcasted_iota(jnp.int32, (rows, n_band), 0) & (tb - 1)
        c = lax.broadcasted_iota(jnp.int32, (rows, n_band), 1)
        in_window = (c - r).astype(jnp.uint32) <= jnp.uint32(2 * WINDOW)
        k_all = [jnp.concatenate([kp_ref[0, kv * LANES:(kv + 1) * LANES, :],
                                  kc_ref[0, kv * LANES:(kv + 1) * LANES, :],
                                  kn_ref[0, kv * LANES:(kv + 1) * LANES, :]], axis=1) for kv in range(n_kv)]
        v_all = [jnp.concatenate([vp_ref[0, :, kv * LANES:(kv + 1) * LANES],
                                  vc_ref[0, :, kv * LANES:(kv + 1) * LANES],
                                  vn_ref[0, :, kv * LANES:(kv + 1) * LANES]], axis=0) for kv in range(n_kv)]
    for sub in range(tq // tb):
        if band:
            ok = in_window
            if sub == 0:
                ok = ok & (c >= jnp.where(j > 0, 0, WINDOW))
            if sub == tq // tb - 1:
                ok = ok & (c < jnp.where(j < n_blocks - 1, n_band, WINDOW + tb))
        q = q_ref[0, sub * tb:(sub + 1) * tb, :]
        for pair in range(n_kv // 2):
            probs, vals, stats = [], [], []
            for kv in (2 * pair, 2 * pair + 1):
                cs = slice(kv * LANES, (kv + 1) * LANES)
                qb0 = q[:, (2 * kv) * LANES:(2 * kv + 1) * LANES]
                qb1 = q[:, (2 * kv + 1) * LANES:(2 * kv + 2) * LANES]
                q4 = jnp.concatenate([qb0 * lo_bf, qb0 * hi_bf, qb1 * lo_bf, qb1 * hi_bf], axis=0)
                sk = jnp.concatenate([jnp.full((tb, 1), sink_ref[WA_GROUP * kv + g] * LOG2E, F32)
                                      for g in range(WA_GROUP)], axis=0)
                s_x = jnp.dot(q4, kx_ref[0, cs, :], preferred_element_type=F32)
                blk_max = _lane_block_max(s_x)
                v_rows = [vx_ref[0, :, cs]]
                if band:
                    s_b = jnp.dot(q4, k_all[kv][:, sub * tb:sub * tb + n_band], preferred_element_type=F32)
                    s_b = jnp.where(ok, s_b, NEG)
                    blk_max = _lane_block_max(s_b, blk_max)
                    v_rows = [v_all[kv][sub * tb:sub * tb + n_band]] + v_rows
                m = jnp.maximum(jnp.max(blk_max, axis=1, keepdims=True), sk)
                p = [jnp.exp2((s_x - m).astype(BF16))]
                if band:
                    p = [jnp.exp2((s_b - m).astype(BF16))] + p
                v_cat = jnp.concatenate(v_rows, axis=0)
                probs.append(jnp.concatenate(p, axis=1))
                vals.append(jnp.where(lo, v_cat, jnp.ones_like(v_cat)))
                stats.append((m, sk))
            zeros = jnp.zeros_like(vals[0])
            v_diag = jnp.concatenate([jnp.concatenate([vals[0], zeros], axis=1),
                                      jnp.concatenate([zeros, vals[1]], axis=1)], axis=0)
            o2 = jnp.dot(jnp.concatenate(probs, axis=1), v_diag, preferred_element_type=F32)
            for i, kv in enumerate((2 * pair, 2 * pair + 1)):
                oa = o2[:, i * LANES:(i + 1) * LANES]
                m, sk = stats[i]
                on = oa / (oa[:, HEAD_DIM:HEAD_DIM + 1] + jnp.exp2(sk - m))
                for half in range(2):
                    even = on[(2 * half) * tb:(2 * half + 1) * tb]
                    odd = on[(2 * half + 1) * tb:(2 * half + 2) * tb]
                    blk = jnp.where(lo, even, pltpu.roll(odd, HEAD_DIM, 1))
                    o_ref[0, sub * tb:(sub + 1) * tb,
                          (2 * kv + half) * LANES:(2 * kv + half + 1) * LANES] = blk.astype(BF16)


WA_TQ = 1024


def _wa_attn(sink, q, kt, v, kxt, vx, band):
    b, s, d = q.shape
    nk = vx.shape[2]
    n_ctx = vx.shape[1]
    smem = pl.BlockSpec(memory_space=pltpu.SMEM)
    if band:
        tq = WA_TQ if s % WA_TQ == 0 else WINDOW
        nb = s // tq
        per = tq // WINDOW
        last = s // WINDOW - 1
        before = lambda j: jnp.maximum(j * per - 1, 0)
        after = lambda j: jnp.minimum((j + 1) * per, last)
        k_specs = [pl.BlockSpec((1, nk, WINDOW), lambda bi, j: (bi, 0, before(j))),
                   pl.BlockSpec((1, nk, tq), lambda bi, j: (bi, 0, j)),
                   pl.BlockSpec((1, nk, WINDOW), lambda bi, j: (bi, 0, after(j)))]
        v_specs = [pl.BlockSpec((1, WINDOW, nk), lambda bi, j: (bi, before(j), 0)),
                   pl.BlockSpec((1, tq, nk), lambda bi, j: (bi, j, 0)),
                   pl.BlockSpec((1, WINDOW, nk), lambda bi, j: (bi, after(j), 0))]
        in_specs = [smem, pl.BlockSpec((1, tq, d), lambda bi, j: (bi, j, 0))] + k_specs + v_specs
        args = [sink, q, kt, kt, kt, v, v, v]
    else:
        tq = _row_tile(s)
        nb = s // tq
        in_specs = [smem, pl.BlockSpec((1, tq, d), lambda bi, j: (bi, j, 0))]
        args = [sink, q]
    in_specs += [pl.BlockSpec((1, nk, n_ctx), lambda bi, j: (bi, 0, 0)),
                 pl.BlockSpec((1, n_ctx, nk), lambda bi, j: (bi, 0, 0))]
    args += [kxt, vx]
    return pl.pallas_call(
        functools.partial(_wa_attn_kernel, n_blocks=nb, band=band),
        grid=(b, nb),
        in_specs=in_specs,
        out_specs=pl.BlockSpec((1, tq, d), lambda bi, j: (bi, j, 0)),
        out_shape=jax.ShapeDtypeStruct((b, s, d), BF16),
        compiler_params=_cparams(("parallel", "parallel")),
        name="window_gqa" if band else "ctx_gqa",
    )(*args)


DA_TK = 512
DA_TQ = 1024


def _da_attn_kernel(lam_ref, subg_ref, q_ref, kt_ref, v_ref, kxt_ref, vx_ref, o_ref, s_scr, *, lam_init):
    tq = q_ref.shape[1]
    s_len = v_ref.shape[1]
    n_ctx = vx_ref.shape[1]
    lane = lax.broadcasted_iota(jnp.int32, (1, LANES), 1)
    lo = lane < HEAD_DIM
    lo_bf = jnp.where(lo, 1.0, 0.0).astype(BF16)
    hi_bf = jnp.where(lo, 0.0, 1.0).astype(BF16)
    q = q_ref[0]
    qq = jnp.concatenate([q * lo_bf, q * hi_bf], axis=0)

    chunks = [(kt_ref, v_ref, r0, min(DA_TK, s_len - r0)) for r0 in range(0, s_len, DA_TK)]
    chunks.append((kxt_ref, vx_ref, 0, n_ctx))

    blk_max = None
    base = 0
    for kref, _, r0, n in chunks:
        s = jnp.dot(qq, kref[0, :, r0:r0 + n], preferred_element_type=F32)
        s_scr[:, base:base + n] = s
        blk_max = _lane_block_max(s, blk_max)
        base += n
    m = jnp.max(blk_max, axis=1, keepdims=True)
    acc = [jnp.zeros((tq, 2 * LANES), F32) for _ in range(2)]
    base = 0
    for _, vref, r0, n in chunks:
        v_aug = jnp.concatenate([vref[0, r0:r0 + n, :], jnp.ones((n, LANES), BF16)], axis=1)
        for comp in range(2):
            rs = slice(comp * tq, (comp + 1) * tq)
            p = jnp.exp2((s_scr[rs, base:base + n] - m[rs]).astype(BF16))
            acc[comp] = acc[comp] + jnp.dot(p, v_aug, preferred_element_type=F32)
        base += n
    on = [a[:, :LANES] / a[:, LANES:] for a in acc]
    lam = (jnp.exp(jnp.sum(lam_ref[0:1, :] * lam_ref[1:2, :], axis=-1, keepdims=True))
           - jnp.exp(jnp.sum(lam_ref[2:3, :] * lam_ref[3:4, :], axis=-1, keepdims=True)) + lam_init)
    o = on[0] - lam * on[1]
    o = o * lax.rsqrt(jnp.mean(o * o, axis=-1, keepdims=True) + EPS) * subg_ref[...]
    o_ref[0] = (o * (1.0 - lam_init)).astype(BF16)


def _da_attn(lam_rows, sub_g, q, kt, v, kxt, vx, lam_init):
    b, s, d = q.shape
    n_ctx = vx.shape[1]
    n_heads = d // LANES
    tq = DA_TQ if s % DA_TQ == 0 else _row_tile(s)
    head_rows = lambda n: pl.BlockSpec((1, n, LANES), lambda bi, hi, qi: (bi, 0, hi))
    head_cols = lambda n: pl.BlockSpec((1, LANES, n), lambda bi, hi, qi: (bi, hi, 0))
    return pl.pallas_call(
        functools.partial(_da_attn_kernel, lam_init=lam_init),
        grid=(b, n_heads, s // tq),
        in_specs=[pl.BlockSpec((SUBLANES, LANES), lambda bi, hi, qi: (0, 0)),
                  pl.BlockSpec((1, LANES), lambda bi, hi, qi: (0, 0)),
                  pl.BlockSpec((1, tq, LANES), lambda bi, hi, qi: (bi, qi, hi)),
                  head_cols(s), head_rows(s), head_cols(n_ctx), head_rows(n_ctx)],
        out_specs=pl.BlockSpec((1, tq, LANES), lambda bi, hi, qi: (bi, qi, hi)),
        out_shape=jax.ShapeDtypeStruct((b, s, d), BF16),
        scratch_shapes=[pltpu.VMEM((2 * tq, s + n_ctx), F32)],
        compiler_params=_cparams(("parallel", "parallel", "arbitrary")),
        name="diff_attn",
    )(lam_rows, sub_g, q, kt, v, kxt, vx)


def _widen_heads(w, n_heads, dup):
    d = w.shape[0]
    w = w.reshape(d, n_heads, HEAD_DIM)
    other = w if dup else jnp.zeros_like(w)
    return jnp.concatenate([w, other], axis=-1).reshape(d, n_heads * LANES)


def _wa_layer(h, hc, mods, mods_c, g1, w_qkv, q_g, k_g, sink, w_out, rope_tabs):
    d = h.shape[2]
    n_q = d // HEAD_DIM
    n_kv = n_q // WA_GROUP
    wq = w_qkv[:, :d]
    wk = _widen_heads(w_qkv[:, d:d + n_kv * HEAD_DIM], n_kv, dup=True)
    wv = _widen_heads(w_qkv[:, d + n_kv * HEAD_DIM:], n_kv, dup=False)
    w = jnp.concatenate([wq, wk, wv], axis=1).astype(BF16)
    nk = n_kv * LANES
    gains = jnp.concatenate([jnp.tile(q_g, n_q) * (HEAD_DIM ** -0.5 * LOG2E),
                             jnp.tile(k_g, 2 * n_kv)])[None, :]
    widths = (d, nk, nk)
    tr = (False, True, False)
    q, k, v = _proj(h, mods, g1, w, gains, widths, tr, d + nk, rope_tabs)
    qc, kc, vc = _proj(hc, mods_c, g1, w, gains, widths, tr, d + nk, None)
    o = _wa_attn(sink, q, k, v, kc, vc, band=True)
    oc = _wa_attn(sink, qc, None, None, kc, vc, band=False)
    return o, oc, w_out.astype(BF16)


def _da_layer(h, hc, mods, mods_c, g1, w_qkv, q_g, k_g, lq1, lk1, lq2, lk2, sub_g, w_out,
              layer_idx, rope_tabs):
    d = h.shape[2]
    n_qk = 2 * d // HEAD_DIM
    w = w_qkv.astype(BF16)
    gains = jnp.concatenate([jnp.tile(q_g, n_qk // 2) * (HEAD_DIM ** -0.5 * LOG2E),
                             jnp.tile(k_g, n_qk // 2)])[None, :]
    widths = (d, d, d)
    tr = (False, True, False)
    q, k, v = _proj(h, mods, g1, w, gains, widths, tr, 2 * d, rope_tabs)
    _, kc, vc = _proj(hc, mods_c, g1, w, gains, widths, tr, 2 * d, None)
    lam_init = 0.8 - 0.6 * math.exp(-0.3 * layer_idx)
    lam_rows = jnp.zeros((SUBLANES, LANES), F32).at[:4, :HEAD_DIM].set(jnp.stack([lq1, lk1, lq2, lk2]))
    o = _da_attn(lam_rows, sub_g[None, :], q, k, v, kc, vc, lam_init)
    return o, w_out.astype(BF16)


def kernel(x, c, ctx, c_ctx, l0_ada_w, l0_ada_b, l0_norm1, l0_norm2, l0_sc_in, l0_sc_conv, l0_sc_out, l0_ffn_up, l0_ffn_conv_w, l0_ffn_conv_b, l0_ffn_down, l1_ada_w, l1_ada_b, l1_norm1, l1_norm2, l1_wa_qkv, l1_wa_qnorm, l1_wa_knorm, l1_wa_sink, l1_wa_out, l1_ffn_up, l1_ffn_conv_w, l1_ffn_conv_b, l1_ffn_down, l2_ada_w, l2_ada_b, l2_norm1, l2_norm2, l2_da_qkv, l2_da_qnorm, l2_da_knorm, l2_da_lq1, l2_da_lk1, l2_da_lq2, l2_da_lk2, l2_da_subln, l2_da_out, l2_ffn_up, l2_ffn_conv_w, l2_ffn_conv_b, l2_ffn_down, l3_ada_w, l3_ada_b, l3_norm1, l3_norm2, l3_sc_in, l3_sc_conv, l3_sc_out, l3_ffn_up, l3_ffn_conv_w, l3_ffn_conv_b, l3_ffn_down):
    commons = [(l0_ada_w, l0_ada_b, l0_norm1, l0_norm2),
               (l1_ada_w, l1_ada_b, l1_norm1, l1_norm2),
               (l2_ada_w, l2_ada_b, l2_norm1, l2_norm2),
               (l3_ada_w, l3_ada_b, l3_norm1, l3_norm2)]
    mixers = [(l0_sc_in, l0_sc_conv, l0_sc_out),
              (l1_wa_qkv, l1_wa_qnorm, l1_wa_knorm, l1_wa_sink, l1_wa_out),
              (l2_da_qkv, l2_da_qnorm, l2_da_knorm, l2_da_lq1, l2_da_lk1, l2_da_lq2, l2_da_lk2,
               l2_da_subln, l2_da_out),
              (l3_sc_in, l3_sc_conv, l3_sc_out)]
    ffns = [(l0_ffn_up, l0_ffn_conv_w, l0_ffn_conv_b, l0_ffn_down),
            (l1_ffn_up, l1_ffn_conv_w, l1_ffn_conv_b, l1_ffn_down),
            (l2_ffn_up, l2_ffn_conv_w, l2_ffn_conv_b, l2_ffn_down),
            (l3_ffn_up, l3_ffn_conv_w, l3_ffn_conv_b, l3_ffn_down)]
    depth = len(commons)
    bsz, seq, d = x.shape
    rope_tabs = _rope_tables(seq)

    n_rows = -(-(bsz + 1) // SUBLANES) * SUBLANES
    cvec = jnp.zeros((n_rows, d), F32).at[:bsz].set(c).at[bsz].set(c_ctx)

    h, hc = x, ctx
    for i in range(depth):
        kind = i % N_MIXERS
        ada_w, ada_b, g1, g2 = commons[i]
        ctx_after = any(j % N_MIXERS != 0 for j in range(i + 1, depth))
        mods_all = _ada(cvec, ada_w, ada_b).reshape(n_rows, 6, d)
        mods, mods_c = mods_all[:bsz], mods_all[bsz:bsz + 1]
        g1, g2 = g1[None, :], g2[None, :]

        attn = attn_c = None
        if kind == 0:
            w_in, w_conv, w_out = mixers[i]
            w_in, w_out = w_in.astype(BF16), w_out.astype(BF16)
            h = _conv_mixer(h, mods, g1, w_in, w_conv, w_out)
            if ctx_after:
                hc = _conv_mixer(hc, mods_c, g1, w_in, w_conv, w_out)
        elif kind == 1:
            o, oc, w_out = _wa_layer(h, hc, mods, mods_c, g1, *mixers[i], rope_tabs)
            attn, attn_c = (o, w_out), (oc, w_out)
        else:
            o, w_out = _da_layer(h, hc, mods, mods_c, g1, *mixers[i], i, rope_tabs)
            attn = (o, w_out)
            assert not ctx_after

        w_up, conv_w, conv_b, w_down = ffns[i]
        w_up, w_down = w_up.astype(BF16), w_down.astype(BF16)
        conv_b = conv_b[None, :]
        h = _ffn(h, mods, g2, w_up, conv_w, conv_b, w_down, attn)
        if ctx_after:
            hc = _ffn(hc, mods_c, g2, w_up, conv_w, conv_b, w_down, attn_c)
    return h
```

```python
import functools
import math

import jax
import jax.numpy as jnp
from jax import lax
from jax.experimental import pallas as pl
from jax.experimental.pallas import tpu as pltpu

F32 = jnp.float32
BF16 = jnp.bfloat16

LANES = 128
SUBLANES = 8
BF16_ROWS = 16
MXU_DIM = 256
VMEM_LIMIT_BYTES = 56 * 1024 * 1024

GRID_W = 64
HEAD_DIM = 64
WA_GROUP = 4
WINDOW = 128
CONV_W = 3
N_MIXERS = 3
ROPE_BASE = 10000.0
EPS = 1e-6
NEG = -1e30

NT_DIMS = (((1,), (1,)), ((), ()))
LOG2E = 1.4426950408889634


def _cparams(sem):
    return pltpu.CompilerParams(dimension_semantics=sem, vmem_limit_bytes=VMEM_LIMIT_BYTES)


def _sigmoid(x):
    return 1.0 / (1.0 + jnp.exp(-x))


def _norm_mod(x, g, shift, scale):
    ms = jnp.mean(x * x, axis=-1, keepdims=True)
    y = x * lax.rsqrt(ms + EPS) * g
    return y * (1.0 + scale) + shift


def _row_tile(s):
    for tm in (512, 256, 128):
        if s % tm == 0:
            return tm
    raise ValueError(f"sequence length {s} must be a multiple of 128")


def _halo_specs(tm, d, s):
    per = tm // SUBLANES
    last = s // SUBLANES - 1
    cur = pl.BlockSpec((1, tm, d), lambda b, t: (b, t, 0))
    prev = pl.BlockSpec((1, SUBLANES, d), lambda b, t: (b, jnp.maximum(t * per - 1, 0), 0))
    nxt = pl.BlockSpec((1, SUBLANES, d), lambda b, t: (b, jnp.minimum((t + 1) * per, last), 0))
    return cur, prev, nxt


def _mod_spec(mods, d):
    if mods.shape[0] == 1:
        return pl.BlockSpec((1, 6, d), lambda b, t: (0, 0, 0))
    return pl.BlockSpec((1, 6, d), lambda b, t: (b, 0, 0))


def _const_spec(shape):
    zeros = (0,) * len(shape)
    return pl.BlockSpec(shape, lambda *_: zeros, pipeline_mode=pl.Buffered(1))


def _ada_kernel(c_ref, w_ref, b_ref, o_ref):
    cv = c_ref[...]
    act = cv * _sigmoid(cv)
    o_ref[...] = jnp.dot(act, w_ref[...], preferred_element_type=F32,
                         precision=lax.Precision.HIGHEST) + b_ref[...]


def _ada(cvec, ada_w, ada_b):
    r, d = cvec.shape
    n = ada_w.shape[1]
    tn = n // 4
    return pl.pallas_call(
        _ada_kernel,
        grid=(n // tn,),
        in_specs=[pl.BlockSpec((r, d), lambda i: (0, 0)),
                  pl.BlockSpec((d, tn), lambda i: (0, i)),
                  pl.BlockSpec((1, tn), lambda i: (0, i))],
        out_specs=pl.BlockSpec((r, tn), lambda i: (0, i)),
        out_shape=jax.ShapeDtypeStruct((r, n), F32),
        compiler_params=_cparams(("arbitrary",)),
        name="ada_mod",
    )(cvec, ada_w, ada_b.reshape(1, n))


def _shift_conv(scr, pieces, tm, w0, w1, w2):
    prev, cur, nxt = pieces
    scr[0:SUBLANES, :] = prev
    scr[SUBLANES:SUBLANES + tm, :] = cur
    scr[SUBLANES + tm:, :] = nxt
    return (w0 * scr[SUBLANES - 1:SUBLANES - 1 + tm, :] + w1 * cur
            + w2 * scr[SUBLANES + 1:SUBLANES + 1 + tm, :])


def _conv_mixer_kernel(h_ref, hp_ref, hn_ref, mod_ref, g_ref, win_ref, wc_ref, wout_ref,
                       o_ref, scr, *, n_tiles):
    t = pl.program_id(1)
    tm, d = h_ref.shape[1], h_ref.shape[2]
    hcur = h_ref[0]
    shift, scale, gate = mod_ref[0, 0:1, :], mod_ref[0, 1:2, :], mod_ref[0, 2:3, :]
    x_ext = jnp.concatenate([hcur, hp_ref[0], hn_ref[0]], axis=0)
    xn = _norm_mod(x_ext, g_ref[...], shift, scale).astype(BF16)
    b_gate = jnp.dot(xn[:tm], win_ref[:, 0:d], preferred_element_type=F32)
    c_gate = jnp.dot(xn, win_ref[:, d:2 * d], preferred_element_type=F32)
    hh = jnp.dot(xn, win_ref[:, 2 * d:3 * d], preferred_element_type=F32)
    ch = c_gate * hh
    prev = jnp.where(t > 0, ch[tm:tm + SUBLANES], 0.0)
    nxt = jnp.where(t < n_tiles - 1, ch[tm + SUBLANES:], 0.0)
    conv = _shift_conv(scr, (prev, ch[:tm], nxt), tm, wc_ref[0:1, :], wc_ref[1:2, :], wc_ref[2:3, :])
    z = (b_gate * conv).astype(BF16)
    y = jnp.dot(z, wout_ref[...], preferred_element_type=F32)
    o_ref[0] = hcur + gate * y


def _conv_mixer(h, mods, g1, w_in, w_conv, w_out):
    b, s, d = h.shape
    tm = _row_tile(s)
    n_tiles = s // tm
    cur, prev, nxt = _halo_specs(tm, d, s)
    return pl.pallas_call(
        functools.partial(_conv_mixer_kernel, n_tiles=n_tiles),
        grid=(b, n_tiles),
        in_specs=[cur, prev, nxt, _mod_spec(mods, d), _const_spec((1, d)),
                  _const_spec((d, 3 * d)), _const_spec((CONV_W, d)), _const_spec((d, d))],
        out_specs=pl.BlockSpec((1, tm, d), lambda bi, t: (bi, t, 0)),
        out_shape=jax.ShapeDtypeStruct((b, s, d), F32),
        scratch_shapes=[pltpu.VMEM((tm + 2 * SUBLANES, d), F32)],
        compiler_params=_cparams(("parallel", "parallel")),
        name="conv_mixer",
    )(h, h, h, mods, g1, w_in, w_conv, w_out)


FFN_CHUNK = 2816


def _ffn_chunks(dff):
    return [(c0, min(FFN_CHUNK, dff - c0)) for c0 in range(0, dff, FFN_CHUNK)]


def _ffn_kernel(*refs, n_tiles, chunks, fused_attn):
    if fused_attn:
        (h_ref, hp_ref, hn_ref, a_ref, ap_ref, an_ref, wo_ref,
         mod_ref, g_ref, wup_ref, cw_ref, cb_ref, wdn_ref, o_ref, scr) = refs
    else:
        h_ref, hp_ref, hn_ref, mod_ref, g_ref, wup_ref, cw_ref, cb_ref, wdn_ref, o_ref, scr = refs
    t = pl.program_id(1)
    tm, d = h_ref.shape[1], h_ref.shape[2]
    dff = wdn_ref.shape[0]
    hcur, hprev, hnext = h_ref[0], hp_ref[0], hn_ref[0]
    if fused_attn:
        a_ext = jnp.concatenate([a_ref[0], ap_ref[0], an_ref[0]], axis=0)
        y = jnp.dot(a_ext, wo_ref[...], preferred_element_type=F32)
        gate1 = mod_ref[0, 2:3, :]
        hcur = hcur + gate1 * y[:tm]
        hprev = hprev + gate1 * y[tm + BF16_ROWS - SUBLANES:tm + BF16_ROWS]
        hnext = hnext + gate1 * y[tm + BF16_ROWS:tm + BF16_ROWS + SUBLANES]
    shift, scale, gate = mod_ref[0, 3:4, :], mod_ref[0, 4:5, :], mod_ref[0, 5:6, :]
    x_ext = jnp.concatenate([hcur, hprev, hnext], axis=0)
    xn = _norm_mod(x_ext, g_ref[...], shift, scale).astype(BF16)
    acc = jnp.zeros((tm, d), F32)
    for c0, fc in chunks:
        a = jnp.dot(xn[:tm], wup_ref[:, c0:c0 + fc], preferred_element_type=F32)
        gg = jnp.dot(xn, wup_ref[:, dff + c0:dff + c0 + fc], preferred_element_type=F32)
        prev = jnp.where(t > 0, gg[tm:tm + SUBLANES], 0.0)
        nxt = jnp.where(t < n_tiles - 1, gg[tm + SUBLANES:], 0.0)
        gc = _shift_conv(scr.at[:, 0:fc], (prev, gg[:tm], nxt), tm, cw_ref[0:1, c0:c0 + fc],
                         cw_ref[1:2, c0:c0 + fc], cw_ref[2:3, c0:c0 + fc]) + cb_ref[:, c0:c0 + fc]
        u = (gc * _sigmoid(gc) * a).astype(BF16)
        acc = acc + jnp.dot(u, wdn_ref[c0:c0 + fc, :], preferred_element_type=F32)
    o_ref[0] = hcur + gate * acc


def _ffn(h, mods, g2, w_up, conv_w, conv_b, w_down, attn=None):
    b, s, d = h.shape
    dff = w_down.shape[0]
    tm = _row_tile(s)
    n_tiles = s // tm
    chunks = _ffn_chunks(dff)
    cur, prev, nxt = _halo_specs(tm, d, s)
    in_specs = [cur, prev, nxt]
    args = [h, h, h]
    if attn is not None:
        o, w_out = attn
        k = o.shape[2]
        per = tm // BF16_ROWS
        last = s // BF16_ROWS - 1
        in_specs += [pl.BlockSpec((1, tm, k), lambda bi, t: (bi, t, 0)),
                     pl.BlockSpec((1, BF16_ROWS, k), lambda bi, t: (bi, jnp.maximum(t * per - 1, 0), 0)),
                     pl.BlockSpec((1, BF16_ROWS, k), lambda bi, t: (bi, jnp.minimum((t + 1) * per, last), 0)),
                     _const_spec((k, d))]
        args += [o, o, o, w_out]
    return pl.pallas_call(
        functools.partial(_ffn_kernel, n_tiles=n_tiles, chunks=tuple(chunks), fused_attn=attn is not None),
        grid=(b, n_tiles),
        in_specs=in_specs + [_mod_spec(mods, d), _const_spec((1, d)),
                             _const_spec((d, 2 * dff)), _const_spec((CONV_W, dff)), _const_spec((1, dff)),
                             _const_spec((dff, d))],
        out_specs=pl.BlockSpec((1, tm, d), lambda bi, t: (bi, t, 0)),
        out_shape=jax.ShapeDtypeStruct((b, s, d), F32),
        scratch_shapes=[pltpu.VMEM((tm + 2 * SUBLANES, max(fc for _, fc in chunks)), F32)],
        compiler_params=_cparams(("parallel", "parallel")),
        name="conv_ffn" if attn is None else "outproj_conv_ffn",
    )(*args, mods, g2, w_up, conv_w, conv_b, w_down)


def _head_sum_matrix():
    i = jnp.arange(MXU_DIM) // HEAD_DIM
    return (i[:, None] == i[None, :]).astype(BF16)


def _rope_tables(n_tokens):
    rows = n_tokens // GRID_W
    row = jnp.repeat(jnp.arange(rows, dtype=F32), GRID_W)
    col = jnp.tile(jnp.arange(GRID_W, dtype=F32), rows)
    m = HEAD_DIM // 4
    inv_freq = ROPE_BASE ** (-jnp.arange(m, dtype=F32) / m)
    ang = jnp.stack([row, col], axis=-1)[:, :, None] * inv_freq
    cos, sin = jnp.cos(ang), jnp.sin(ang)
    c64 = jnp.concatenate([cos[:, 0], cos[:, 0], cos[:, 1], cos[:, 1]], axis=-1)
    s64 = jnp.concatenate([-sin[:, 0], sin[:, 0], -sin[:, 1], sin[:, 1]], axis=-1)
    reps = LANES // HEAD_DIM
    return jnp.tile(c64, (1, reps)), jnp.tile(s64, (1, reps))


PROJ_SUBTILES = 4


def _proj_kernel(*refs, n_norm, widths, transposed, rope):
    h_ref, mod_ref, g_ref, w_ref, bm_ref, gain_ref = refs[:6]
    pos = 6
    if rope:
        cos_ref, sin_ref = refs[6:8]
        pos = 8
    out_refs = refs[pos:]
    shift, scale = mod_ref[0, 0:1, :], mod_ref[0, 1:2, :]
    starts = [sum(widths[:i]) for i in range(len(widths))]
    if rope:
        lane = lax.broadcasted_iota(jnp.int32, (1, LANES), 1)
        upper = (lane & (HEAD_DIM // 4)) != 0
    tm = h_ref.shape[1]
    sub = tm // PROJ_SUBTILES if tm % (PROJ_SUBTILES * LANES) == 0 else tm
    for r0 in range(0, tm, sub):
        rows = slice(r0, r0 + sub)
        xn = _norm_mod(h_ref[0, rows, :], g_ref[...], shift, scale).astype(BF16)
        u = jnp.dot(xn, w_ref[...], preferred_element_type=F32)

        def store(c0, val):
            for ref, st, wd, tr in zip(out_refs, starts, widths, transposed):
                if st <= c0 < st + wd and tr:
                    ref[0, c0 - st:c0 - st + val.shape[1], rows] = val.T.astype(BF16)
                elif st <= c0 < st + wd:
                    ref[0, rows, c0 - st:c0 - st + val.shape[1]] = val.astype(BF16)

        if rope:
            cos_t, sin_t = cos_ref[rows, :], sin_ref[rows, :]
        for c0 in range(0, n_norm, MXU_DIM):
            xb = u[:, c0:c0 + MXU_DIM]
            ss = jnp.dot((xb * xb).astype(BF16), bm_ref[...], preferred_element_type=F32)
            xb = xb * lax.rsqrt(ss * (1.0 / HEAD_DIM) + EPS) * gain_ref[:, c0:c0 + MXU_DIM]
            for c1 in range(0, MXU_DIM, LANES):
                xs = xb[:, c1:c1 + LANES]
                if rope:
                    partner = jnp.where(upper, pltpu.roll(xs, HEAD_DIM // 4, 1),
                                        pltpu.roll(xs, LANES - HEAD_DIM // 4, 1))
                    xs = xs * cos_t + partner * sin_t
                store(c0 + c1, xs)
        for c0 in range(n_norm, sum(widths), LANES):
            store(c0, u[:, c0:c0 + LANES])


def _proj(h, mods, g1, w, gains, widths, transposed, n_norm, rope_tabs):
    b, s, d = h.shape
    n = w.shape[1]
    tm = _row_tile(s)
    rope = rope_tabs is not None
    in_specs = [pl.BlockSpec((1, tm, d), lambda bi, t: (bi, t, 0)), _mod_spec(mods, d),
                _const_spec((1, d)), _const_spec((d, n)), _const_spec((MXU_DIM, MXU_DIM)),
                _const_spec((1, n_norm))]
    args = [h, mods, g1, w, _head_sum_matrix(), gains]
    if rope:
        in_specs += [pl.BlockSpec((tm, LANES), lambda bi, t: (t, 0))] * 2
        args += list(rope_tabs)
    out_specs = [pl.BlockSpec((1, wd, tm), lambda bi, t: (bi, 0, t)) if tr
                 else pl.BlockSpec((1, tm, wd), lambda bi, t: (bi, t, 0)) for wd, tr in zip(widths, transposed)]
    out_shape = [jax.ShapeDtypeStruct((b, wd, s) if tr else (b, s, wd), BF16)
                 for wd, tr in zip(widths, transposed)]
    return pl.pallas_call(
        functools.partial(_proj_kernel, n_norm=n_norm, widths=tuple(widths),
                          transposed=tuple(transposed), rope=rope),
        grid=(b, s // tm),
        in_specs=in_specs,
        out_specs=out_specs,
        out_shape=out_shape,
        compiler_params=_cparams(("parallel", "parallel")),
        name="qkv_proj_rope" if rope else "qkv_proj",
    )(*args)


def _lane_block_max(x, acc=None):
    for c0 in range(0, x.shape[1], LANES):
        blk = x[:, c0:c0 + LANES]
        acc = blk if acc is None else jnp.maximum(acc, blk)
    return acc


def _wa_attn_kernel(*refs, n_blocks, band):
    if band:
        sink_ref, q_ref, kp_ref, kc_ref, kn_ref, vp_ref, vc_ref, vn_ref, kx_ref, vx_ref, o_ref = refs
    else:
        sink_ref, q_ref, kx_ref, vx_ref, o_ref = refs
    j = pl.program_id(1)
    tq = q_ref.shape[1]
    tb = WINDOW
    n_kv = vx_ref.shape[2] // LANES
    rows = WA_GROUP * tb
    lane = lax.broadcasted_iota(jnp.int32, (1, LANES), 1)
    lo = lane < HEAD_DIM
    lo_bf = jnp.where(lo, 1.0, 0.0).astype(BF16)
    hi_bf = jnp.where(lo, 0.0, 1.0).astype(BF16)
    if band:
        n_band = tb + 2 * WINDOW
        r = lax.broadcasted_iota(jnp.int32, (rows, n_band), 0) & (tb - 1)
        c = lax.broadcasted_iota(jnp.int32, (rows, n_band), 1)
        in_window = (c - r).astype(jnp.uint32) <= jnp.uint32(2 * WINDOW)
        k_all = [jnp.concatenate([kp_ref[0, kv * LANES:(kv + 1) * LANES, :],
                                  kc_ref[0, kv * LANES:(kv + 1) * LANES, :],
                                  kn_ref[0, kv * LANES:(kv + 1) * LANES, :]], axis=1) for kv in range(n_kv)]
        v_all = [jnp.concatenate([vp_ref[0, :, kv * LANES:(kv + 1) * LANES],
                                  vc_ref[0, :, kv * LANES:(kv + 1) * LANES],
                                  vn_ref[0, :, kv * LANES:(kv + 1) * LANES]], axis=0) for kv in range(n_kv)]
    for sub in range(tq // tb):
        if band:
            ok = in_window
            if sub == 0:
                ok = ok & (c >= jnp.where(j > 0, 0, WINDOW))
            if sub == tq // tb - 1:
                ok = ok & (c < jnp.where(j < n_blocks - 1, n_band, WINDOW + tb))
        q = q_ref[0, sub * tb:(sub + 1) * tb, :]
        for pair in range(n_kv // 2):
            probs, vals, stats = [], [], []
            for kv in (2 * pair, 2 * pair + 1):
                cs = slice(kv * LANES, (kv + 1) * LANES)
                qb0 = q[:, (2 * kv) * LANES:(2 * kv + 1) * LANES]
                qb1 = q[:, (2 * kv + 1) * LANES:(2 * kv + 2) * LANES]
                q4 = jnp.concatenate([qb0 * lo_bf, qb0 * hi_bf, qb1 * lo_bf, qb1 * hi_bf], axis=0)
                sk = jnp.concatenate([jnp.full((tb, 1), sink_ref[WA_GROUP * kv + g] * LOG2E, F32)
                                      for g in range(WA_GROUP)], axis=0)
                s_x = jnp.dot(q4, kx_ref[0, cs, :], preferred_element_type=F32)
                blk_max = _lane_block_max(s_x)
                v_rows = [vx_ref[0, :, cs]]
                if band:
                    s_b = jnp.dot(q4, k_all[kv][:, sub * tb:sub * tb + n_band], preferred_element_type=F32)
                    s_b = jnp.where(ok, s_b, NEG)
                    blk_max = _lane_block_max(s_b, blk_max)
                    v_rows = [v_all[kv][sub * tb:sub * tb + n_band]] + v_rows
                m = jnp.maximum(jnp.max(blk_max, axis=1, keepdims=True), sk)
                p = [jnp.exp2((s_x - m).astype(BF16))]
                if band:
                    p = [jnp.exp2((s_b - m).astype(BF16))] + p
                v_cat = jnp.concatenate(v_rows, axis=0)
                probs.append(jnp.concatenate(p, axis=1))
                vals.append(jnp.where(lo, v_cat, jnp.ones_like(v_cat)))
                stats.append((m, sk))
            zeros = jnp.zeros_like(vals[0])
            v_diag = jnp.concatenate([jnp.concatenate([vals[0], zeros], axis=1),
                                      jnp.concatenate([zeros, vals[1]], axis=1)], axis=0)
            o2 = jnp.dot(jnp.concatenate(probs, axis=1), v_diag, preferred_element_type=F32)
            for i, kv in enumerate((2 * pair, 2 * pair + 1)):
                oa = o2[:, i * LANES:(i + 1) * LANES]
                m, sk = stats[i]
                on = oa / (oa[:, HEAD_DIM:HEAD_DIM + 1] + jnp.exp2(sk - m))
                for half in range(2):
                    even = on[(2 * half) * tb:(2 * half + 1) * tb]
                    odd = on[(2 * half + 1) * tb:(2 * half + 2) * tb]
                    blk = jnp.where(lo, even, pltpu.roll(odd, HEAD_DIM, 1))
                    o_ref[0, sub * tb:(sub + 1) * tb,
                          (2 * kv + half) * LANES:(2 * kv + half + 1) * LANES] = blk.astype(BF16)


WA_TQ = 1024


def _wa_attn(sink, q, kt, v, kxt, vx, band):
    b, s, d = q.shape
    nk = vx.shape[2]
    n_ctx = vx.shape[1]
    smem = pl.BlockSpec(memory_space=pltpu.SMEM)
    if band:
        tq = WA_TQ if s % WA_TQ == 0 else WINDOW
        nb = s // tq
        per = tq // WINDOW
        last = s // WINDOW - 1
        before = lambda j: jnp.maximum(j * per - 1, 0)
        after = lambda j: jnp.minimum((j + 1) * per, last)
        k_specs = [pl.BlockSpec((1, nk, WINDOW), lambda bi, j: (bi, 0, before(j))),
                   pl.BlockSpec((1, nk, tq), lambda bi, j: (bi, 0, j)),
                   pl.BlockSpec((1, nk, WINDOW), lambda bi, j: (bi, 0, after(j)))]
        v_specs = [pl.BlockSpec((1, WINDOW, nk), lambda bi, j: (bi, before(j), 0)),
                   pl.BlockSpec((1, tq, nk), lambda bi, j: (bi, j, 0)),
                   pl.BlockSpec((1, WINDOW, nk), lambda bi, j: (bi, after(j), 0))]
        in_specs = [smem, pl.BlockSpec((1, tq, d), lambda bi, j: (bi, j, 0))] + k_specs + v_specs
        args = [sink, q, kt, kt, kt, v, v, v]
    else:
        tq = _row_tile(s)
        nb = s // tq
        in_specs = [smem, pl.BlockSpec((1, tq, d), lambda bi, j: (bi, j, 0))]
        args = [sink, q]
    in_specs += [pl.BlockSpec((1, nk, n_ctx), lambda bi, j: (bi, 0, 0)),
                 pl.BlockSpec((1, n_ctx, nk), lambda bi, j: (bi, 0, 0))]
    args += [kxt, vx]
    return pl.pallas_call(
        functools.partial(_wa_attn_kernel, n_blocks=nb, band=band),
        grid=(b, nb),
        in_specs=in_specs,
        out_specs=pl.BlockSpec((1, tq, d), lambda bi, j: (bi, j, 0)),
        out_shape=jax.ShapeDtypeStruct((b, s, d), BF16),
        compiler_params=_cparams(("parallel", "parallel")),
        name="window_gqa" if band else "ctx_gqa",
    )(*args)


DA_TK = 512
DA_TQ = 1024


def _da_attn_kernel(lam_ref, subg_ref, q_ref, kt_ref, v_ref, kxt_ref, vx_ref, o_ref, s_scr, *, lam_init):
    tq = q_ref.shape[1]
    s_len = v_ref.shape[1]
    n_ctx = vx_ref.shape[1]
    lane = lax.broadcasted_iota(jnp.int32, (1, LANES), 1)
    lo = lane < HEAD_DIM
    lo_bf = jnp.where(lo, 1.0, 0.0).astype(BF16)
    hi_bf = jnp.where(lo, 0.0, 1.0).astype(BF16)
    q = q_ref[0]
    qq = jnp.concatenate([q * lo_bf, q * hi_bf], axis=0)

    chunks = [(kt_ref, v_ref, r0, min(DA_TK, s_len - r0)) for r0 in range(0, s_len, DA_TK)]
    chunks.append((kxt_ref, vx_ref, 0, n_ctx))

    blk_max = None
    base = 0
    for kref, _, r0, n in chunks:
        s = jnp.dot(qq, kref[0, :, r0:r0 + n], preferred_element_type=F32)
        s_scr[:, base:base + n] = s
        blk_max = _lane_block_max(s, blk_max)
        base += n
    m = jnp.max(blk_max, axis=1, keepdims=True)
    acc = [jnp.zeros((tq, 2 * LANES), F32) for _ in range(2)]
    base = 0
    for _, vref, r0, n in chunks:
        v_aug = jnp.concatenate([vref[0, r0:r0 + n, :], jnp.ones((n, LANES), BF16)], axis=1)
        for comp in range(2):
            rs = slice(comp * tq, (comp + 1) * tq)
            p = jnp.exp2((s_scr[rs, base:base + n] - m[rs]).astype(BF16))
            acc[comp] = acc[comp] + jnp.dot(p, v_aug, preferred_element_type=F32)
        base += n
    on = [a[:, :LANES] / a[:, LANES:] for a in acc]
    lam = (jnp.exp(jnp.sum(lam_ref[0:1, :] * lam_ref[1:2, :], axis=-1, keepdims=True))
           - jnp.exp(jnp.sum(lam_ref[2:3, :] * lam_ref[3:4, :], axis=-1, keepdims=True)) + lam_init)
    o = on[0] - lam * on[1]
    o = o * lax.rsqrt(jnp.mean(o * o, axis=-1, keepdims=True) + EPS) * subg_ref[...]
    o_ref[0] = (o * (1.0 - lam_init)).astype(BF16)


def _da_attn(lam_rows, sub_g, q, kt, v, kxt, vx, lam_init):
    b, s, d = q.shape
    n_ctx = vx.shape[1]
    n_heads = d // LANES
    tq = DA_TQ if s % DA_TQ == 0 else _row_tile(s)
    head_rows = lambda n: pl.BlockSpec((1, n, LANES), lambda bi, hi, qi: (bi, 0, hi))
    head_cols = lambda n: pl.BlockSpec((1, LANES, n), lambda bi, hi, qi: (bi, hi, 0))
    return pl.pallas_call(
        functools.partial(_da_attn_kernel, lam_init=lam_init),
        grid=(b, n_heads, s // tq),
        in_specs=[pl.BlockSpec((SUBLANES, LANES), lambda bi, hi, qi: (0, 0)),
                  pl.BlockSpec((1, LANES), lambda bi, hi, qi: (0, 0)),
                  pl.BlockSpec((1, tq, LANES), lambda bi, hi, qi: (bi, qi, hi)),
                  head_cols(s), head_rows(s), head_cols(n_ctx), head_rows(n_ctx)],
        out_specs=pl.BlockSpec((1, tq, LANES), lambda bi, hi, qi: (bi, qi, hi)),
        out_shape=jax.ShapeDtypeStruct((b, s, d), BF16),
        scratch_shapes=[pltpu.VMEM((2 * tq, s + n_ctx), F32)],
        compiler_params=_cparams(("parallel", "parallel", "arbitrary")),
        name="diff_attn",
    )(lam_rows, sub_g, q, kt, v, kxt, vx)


def _widen_heads(w, n_heads, dup):
    d = w.shape[0]
    w = w.reshape(d, n_heads, HEAD_DIM)
    other = w if dup else jnp.zeros_like(w)
    return jnp.concatenate([w, other], axis=-1).reshape(d, n_heads * LANES)


def _wa_layer(h, hc, mods, mods_c, g1, w_qkv, q_g, k_g, sink, w_out, rope_tabs):
    d = h.shape[2]
    n_q = d // HEAD_DIM
    n_kv = n_q // WA_GROUP
    wq = w_qkv[:, :d]
    wk = _widen_heads(w_qkv[:, d:d + n_kv * HEAD_DIM], n_kv, dup=True)
    wv = _widen_heads(w_qkv[:, d + n_kv * HEAD_DIM:], n_kv, dup=False)
    w = jnp.concatenate([wq, wk, wv], axis=1).astype(BF16)
    nk = n_kv * LANES
    gains = jnp.concatenate([jnp.tile(q_g, n_q) * (HEAD_DIM ** -0.5 * LOG2E),
                             jnp.tile(k_g, 2 * n_kv)])[None, :]
    widths = (d, nk, nk)
    tr = (False, True, False)
    q, k, v = _proj(h, mods, g1, w, gains, widths, tr, d + nk, rope_tabs)
    qc, kc, vc = _proj(hc, mods_c, g1, w, gains, widths, tr, d + nk, None)
    o = _wa_attn(sink, q, k, v, kc, vc, band=True)
    oc = _wa_attn(sink, qc, None, None, kc, vc, band=False)
    return o, oc, w_out.astype(BF16)


def _da_layer(h, hc, mods, mods_c, g1, w_qkv, q_g, k_g, lq1, lk1, lq2, lk2, sub_g, w_out,
              layer_idx, rope_tabs):
    d = h.shape[2]
    n_qk = 2 * d // HEAD_DIM
    w = w_qkv.astype(BF16)
    gains = jnp.concatenate([jnp.tile(q_g, n_qk // 2) * (HEAD_DIM ** -0.5 * LOG2E),
                             jnp.tile(k_g, n_qk // 2)])[None, :]
    widths = (d, d, d)
    tr = (False, True, False)
    q, k, v = _proj(h, mods, g1, w, gains, widths, tr, 2 * d, rope_tabs)
    _, kc, vc = _proj(hc, mods_c, g1, w, gains, widths, tr, 2 * d, None)
    lam_init = 0.8 - 0.6 * math.exp(-0.3 * layer_idx)
    lam_rows = jnp.zeros((SUBLANES, LANES), F32).at[:4, :HEAD_DIM].set(jnp.stack([lq1, lk1, lq2, lk2]))
    o = _da_attn(lam_rows, sub_g[None, :], q, k, v, kc, vc, lam_init)
    return o, w_out.astype(BF16)


def kernel(x, c, ctx, c_ctx, l0_ada_w, l0_ada_b, l0_norm1, l0_norm2, l0_sc_in, l0_sc_conv, l0_sc_out, l0_ffn_up, l0_ffn_conv_w, l0_ffn_conv_b, l0_ffn_down, l1_ada_w, l1_ada_b, l1_norm1, l1_norm2, l1_wa_qkv, l1_wa_qnorm, l1_wa_knorm, l1_wa_sink, l1_wa_out, l1_ffn_up, l1_ffn_conv_w, l1_ffn_conv_b, l1_ffn_down, l2_ada_w, l2_ada_b, l2_norm1, l2_norm2, l2_da_qkv, l2_da_qnorm, l2_da_knorm, l2_da_lq1, l2_da_lk1, l2_da_lq2, l2_da_lk2, l2_da_subln, l2_da_out, l2_ffn_up, l2_ffn_conv_w, l2_ffn_conv_b, l2_ffn_down, l3_ada_w, l3_ada_b, l3_norm1, l3_norm2, l3_sc_in, l3_sc_conv, l3_sc_out, l3_ffn_up, l3_ffn_conv_w, l3_ffn_conv_b, l3_ffn_down):
    commons = [(l0_ada_w, l0_ada_b, l0_norm1, l0_norm2),
               (l1_ada_w, l1_ada_b, l1_norm1, l1_norm2),
               (l2_ada_w, l2_ada_b, l2_norm1, l2_norm2),
               (l3_ada_w, l3_ada_b, l3_norm1, l3_norm2)]
    mixers = [(l0_sc_in, l0_sc_conv, l0_sc_out),
              (l1_wa_qkv, l1_wa_qnorm, l1_wa_knorm, l1_wa_sink, l1_wa_out),
              (l2_da_qkv, l2_da_qnorm, l2_da_knorm, l2_da_lq1, l2_da_lk1, l2_da_lq2, l2_da_lk2,
               l2_da_subln, l2_da_out),
              (l3_sc_in, l3_sc_conv, l3_sc_out)]
    ffns = [(l0_ffn_up, l0_ffn_conv_w, l0_ffn_conv_b, l0_ffn_down),
            (l1_ffn_up, l1_ffn_conv_w, l1_ffn_conv_b, l1_ffn_down),
            (l2_ffn_up, l2_ffn_conv_w, l2_ffn_conv_b, l2_ffn_down),
            (l3_ffn_up, l3_ffn_conv_w, l3_ffn_conv_b, l3_ffn_down)]
    depth = len(commons)
    bsz, seq, d = x.shape
    rope_tabs = _rope_tables(seq)

    n_rows = -(-(bsz + 1) // SUBLANES) * SUBLANES
    cvec = jnp.zeros((n_rows, d), F32).at[:bsz].set(c).at[bsz].set(c_ctx)

    h, hc = x, ctx
    for i in range(depth):
        kind = i % N_MIXERS
        ada_w, ada_b, g1, g2 = commons[i]
        ctx_after = any(j % N_MIXERS != 0 for j in range(i + 1, depth))
        mods_all = _ada(cvec, ada_w, ada_b).reshape(n_rows, 6, d)
        mods, mods_c = mods_all[:bsz], mods_all[bsz:bsz + 1]
        g1, g2 = g1[None, :], g2[None, :]

        attn = attn_c = None
        if kind == 0:
            w_in, w_conv, w_out = mixers[i]
            w_in, w_out = w_in.astype(BF16), w_out.astype(BF16)
            h = _conv_mixer(h, mods, g1, w_in, w_conv, w_out)
            if ctx_after:
                hc = _conv_mixer(hc, mods_c, g1, w_in, w_conv, w_out)
        elif kind == 1:
            o, oc, w_out = _wa_layer(h, hc, mods, mods_c, g1, *mixers[i], rope_tabs)
            attn, attn_c = (o, w_out), (oc, w_out)
        else:
            o, w_out = _da_layer(h, hc, mods, mods_c, g1, *mixers[i], i, rope_tabs)
            attn = (o, w_out)
            assert not ctx_after

        w_up, conv_w, conv_b, w_down = ffns[i]
        w_up, w_down = w_up.astype(BF16), w_down.astype(BF16)
        conv_b = conv_b[None, :]
        h = _ffn(h, mods, g2, w_up, conv_w, conv_b, w_down, attn)
        if ctx_after:
            hc = _ffn(hc, mods_c, g2, w_up, conv_w, conv_b, w_down, attn_c)
    return h
```
